```python
import math
import jax
import jax.numpy as jnp
from jax import lax
import numpy as np

D_MODEL = 1024
BATCH = 4
SEQ = 8192
DEPTH = 2

GRID_W = 64
CTX_LEN = 256
HEAD_DIM = 64
RWKV_HEADS = 8
RWKV_WIDTH = RWKV_HEADS * HEAD_DIM
LORA_DECAY = 64
LORA_ICLR = 64
LORA_GATE = 160
N_DIRS = 2
SHORT_CONV = 3
DECAY_SCALE = math.exp(-0.5)
GN_EPS = 64e-5
Q_HEADS = 8
KV_HEADS = 2
Q_PER_KV = Q_HEADS // KV_HEADS
ATTN_WIDTH = Q_HEADS * HEAD_DIM
KV_WIDTH = KV_HEADS * HEAD_DIM
ATTN_SCALE = HEAD_DIM ** -0.5
Q_BLOCK = 128
ROPE_BASE = 10000.0
ROPE_PAIRS = HEAD_DIM // 4
N_EXPERTS = 16
EC_CAPACITY = 2
EXPERT_FF = ((8 * D_MODEL // 3 + 127) // 128) * 128
NORM_EPS = 1e-6
RWKV_COLS = 3 * RWKV_WIDTH + LORA_GATE + N_DIRS * (LORA_DECAY + LORA_ICLR)
GQA_COLS = ATTN_WIDTH + 2 * KV_WIDTH
GATE_COLS = 2 * D_MODEL
IN_COLS = RWKV_COLS + GQA_COLS + GATE_COLS

kernel_name = 'hybrid_rwkv7_gqa_ecmoe_dit'


def rms_norm(x, g):
    xf = x.astype(jnp.float32)
    y = xf * lax.rsqrt(jnp.mean(xf * xf, axis=-1, keepdims=True) + NORM_EPS)
    return (y * g.astype(jnp.float32)).astype(x.dtype)


def modulate(x, shift, scale):
    return x * (1 + scale) + shift


def centred_conv(u, w):
    up = jnp.pad(u, ((0, 0), (1, 1), (0, 0)))
    return up[:, :-2] * w[0] + up[:, 1:-1] * w[1] + up[:, 2:] * w[2]


def axial_rope_tables(n_tokens):
    rows = n_tokens // GRID_W
    row = jnp.repeat(jnp.arange(rows, dtype=jnp.float32), GRID_W)
    col = jnp.broadcast_to(jnp.arange(GRID_W, dtype=jnp.float32)[None, :], (rows, GRID_W)).reshape(-1)
    inv = jnp.power(ROPE_BASE, -jnp.arange(ROPE_PAIRS, dtype=jnp.float32) / ROPE_PAIRS)
    ang = jnp.stack([row[:, None] * inv, col[:, None] * inv], axis=1)
    return jnp.cos(ang), jnp.sin(ang)


def apply_axial_rope(x, cos, sin):
    shp = x.shape
    xr = x.reshape(shp[:-1] + (2, 2, ROPE_PAIRS))
    expand = (1, shp[1]) + (1,) * (x.ndim - 3) + (2, ROPE_PAIRS)
    cs = cos.reshape(expand).astype(x.dtype)
    sn = sin.reshape(expand).astype(x.dtype)
    x1, x2 = xr[..., 0, :], xr[..., 1, :]
    out = jnp.stack([x1 * cs - x2 * sn, x2 * cs + x1 * sn], axis=-2)
    return out.reshape(shp)


def wkv_scan(s0, r, decay, kk, ka, k_rep, v, reverse):
    def step(S, inp):
        r_t, w_t, kk_t, ka_t, kr_t, v_t = inp
        removed = jnp.einsum('bhvk,bhk->bhv', S, kk_t)
        S = (S * w_t[:, :, None, :]
             - removed[..., None] * ka_t[:, :, None, :]
             + v_t[..., None] * kr_t[:, :, None, :])
        return S, jnp.einsum('bhvk,bhk->bhv', S, r_t)
    xs = tuple(jnp.swapaxes(t, 0, 1) for t in (r, decay, kk, ka, k_rep, v))
    s_end, y = lax.scan(step, s0, xs, reverse=reverse)
    return s_end, jnp.swapaxes(y, 0, 1)


def rwkv_branch(u, s_init, conv_w, decay_w0, decay_w2, iclr_a0, iclr_a2, gate_g2,
                key_k, key_a, bonus_rho, gn_g, gn_b):
    B, T, _ = u.shape
    dt = u.dtype
    f32 = jnp.float32
    u = centred_conv(u, conv_w)
    W = RWKV_WIDTH
    r, k, v = u[..., :W], u[..., W:2 * W], u[..., 2 * W:3 * W]
    o = 3 * W
    g_lo = u[..., o:o + LORA_GATE]
    o += LORA_GATE
    d_lo = u[..., o:o + N_DIRS * LORA_DECAY].reshape(B, T, N_DIRS, LORA_DECAY)
    o += N_DIRS * LORA_DECAY
    a_lo = u[..., o:o + N_DIRS * LORA_ICLR].reshape(B, T, N_DIRS, LORA_ICLR)
    g = (jax.nn.sigmoid(g_lo) @ gate_g2).astype(f32)
    d = (decay_w0 + jnp.einsum('btzl,zlc->btzc', jnp.tanh(d_lo), decay_w2)).astype(f32)
    decay = jnp.exp(-DECAY_SCALE * jax.nn.sigmoid(d))
    a = jax.nn.sigmoid((iclr_a0 + jnp.einsum('btzl,zlc->btzc', a_lo, iclr_a2)).astype(f32))
    kf = k.astype(f32)
    heads = lambda t: t.reshape(t.shape[:-1] + (RWKV_HEADS, HEAD_DIM))
    kk = heads(kf * key_k)
    kk = kk * lax.rsqrt(jnp.sum(kk * kk, axis=-1, keepdims=True) + 1e-12)
    k_rep = heads(kf[:, :, None, :] * (1 + (a - 1) * key_a))
    decay, a = heads(decay), heads(a)
    rf, vf = heads(r.astype(f32)), heads(v.astype(f32))
    ys, bonuses, ends = [], [], []
    for z in range(N_DIRS):
        s_end, yz = wkv_scan(s_init[z], rf, decay[:, :, z], kk, kk * a[:, :, z],
                             k_rep[:, :, z], vf, reverse=(z == 1))
        ys.append(yz)
        ends.append(s_end)
        bonuses.append(jnp.sum(rf * k_rep[:, :, z] * bonus_rho, axis=-1, keepdims=True))
    y = ys[0] + ys[1]
    mu = jnp.mean(y, axis=-1, keepdims=True)
    var = jnp.mean(jnp.square(y - mu), axis=-1, keepdims=True)
    yn = (y - mu) * lax.rsqrt(var + GN_EPS)
    yn = yn * heads(gn_g.astype(f32)) + heads(gn_b.astype(f32))
    out = (yn + (bonuses[0] + bonuses[1]) * vf).reshape(B, T, W) * g
    return out.astype(dt), jnp.stack(ends, axis=0)


def gqa_project(u, q_norm, k_norm):
    B, T, _ = u.shape
    q = rms_norm(u[..., :ATTN_WIDTH].reshape(B, T, KV_HEADS, Q_PER_KV, HEAD_DIM), q_norm)
    k = rms_norm(u[..., ATTN_WIDTH:ATTN_WIDTH + KV_WIDTH].reshape(B, T, KV_HEADS, HEAD_DIM), k_norm)
    v = u[..., ATTN_WIDTH + KV_WIDTH:].reshape(B, T, KV_HEADS, HEAD_DIM)
    return q, k, v


def latent_attention(q, k, v, kc, vc):
    B, T = q.shape[:2]
    nb = T // Q_BLOCK
    qb = jnp.moveaxis(q.reshape(B, nb, Q_BLOCK, KV_HEADS, Q_PER_KV, HEAD_DIM), 1, 0)

    def block(qi):
        s = jnp.concatenate([jnp.einsum('bqkgd,bskd->bkgqs', qi, k),
                             jnp.einsum('bqkgd,bckd->bkgqc', qi, kc)], axis=-1)
        p = jax.nn.softmax(s.astype(jnp.float32), axis=-1).astype(v.dtype)
        return (jnp.einsum('bkgqs,bskd->bqkgd', p[..., :T], v)
                + jnp.einsum('bkgqc,bckd->bqkgd', p[..., T:], vc))

    o = lax.map(block, qb)
    return jnp.moveaxis(o, 0, 1).reshape(B, T, ATTN_WIDTH)


def context_attention(qc, kc, vc):
    B, L = qc.shape[:2]
    s = jnp.einsum('bqkgd,bckd->bkgqc', qc, kc)
    p = jax.nn.softmax(s.astype(jnp.float32), axis=-1).astype(vc.dtype)
    return jnp.einsum('bkgqc,bckd->bqkgd', p, vc).reshape(B, L, ATTN_WIDTH)


def gated_merge(u_gate, y_rwkv, y_attn, w_branch_rwkv, w_branch_attn, w_out):
    g_rwkv = jax.nn.sigmoid(u_gate[..., :D_MODEL])
    g_attn = jax.nn.sigmoid(u_gate[..., D_MODEL:])
    return (g_rwkv * (y_rwkv @ w_branch_rwkv) + g_attn * (y_attn @ w_branch_attn)) @ w_out


def token_mixer(ul, uc, cos, sin, conv_w, decay_w0, decay_w2, iclr_a0, iclr_a2, gate_g2,
                key_k, key_a, bonus_rho, gn_g, gn_b, q_norm, k_norm,
                w_branch_rwkv, w_branch_attn, w_out, need_ctx):
    r_end, a_end = RWKV_COLS, RWKV_COLS + GQA_COLS
    rw = (conv_w, decay_w0, decay_w2, iclr_a0, iclr_a2, gate_g2, key_k, key_a, bonus_rho, gn_g, gn_b)
    B = ul.shape[0]
    s0 = jnp.zeros((N_DIRS, B, RWKV_HEADS, HEAD_DIM, HEAD_DIM), jnp.float32)
    yc_rwkv, s_ctx = rwkv_branch(uc[..., :r_end], s0, *rw)
    yl_rwkv, _ = rwkv_branch(ul[..., :r_end], s_ctx, *rw)
    ql, kl, vl = gqa_project(ul[..., r_end:a_end], q_norm, k_norm)
    qc, kc, vc = gqa_project(uc[..., r_end:a_end], q_norm, k_norm)
    ql = apply_axial_rope(ql, cos, sin) * ATTN_SCALE
    kl = apply_axial_rope(kl, cos, sin)
    yl_attn = latent_attention(ql, kl, vl, kc, vc)
    yl = gated_merge(ul[..., a_end:], yl_rwkv, yl_attn, w_branch_rwkv, w_branch_attn, w_out)
    if need_ctx:
        yc_attn = context_attention(qc * ATTN_SCALE, kc, vc)
        yc = gated_merge(uc[..., a_end:], yc_rwkv, yc_attn, w_branch_rwkv, w_branch_attn, w_out)
    else:
        yc = None
    return yl, yc


def expert_choice_ffn(h, w_router, w_gate, w_up, w_down):
    B, T, D = h.shape
    cap = EC_CAPACITY * T // N_EXPERTS
    aff = jax.nn.softmax((h @ w_router).astype(jnp.float32), axis=-1)
    gate, idx = lax.top_k(jnp.swapaxes(aff, 1, 2), cap)
    xe = jax.vmap(lambda hb, ib: hb[ib])(h, idx)
    hid = jax.nn.silu(jnp.einsum('becd,edf->becf', xe, w_gate)) * jnp.einsum('becd,edf->becf', xe, w_up)
    ye = jnp.einsum('becf,efd->becd', hid, w_down) * gate[..., None].astype(h.dtype)
    return jax.vmap(lambda yb, ib: jnp.zeros((T, D), yb.dtype).at[ib.reshape(-1)].add(yb.reshape(-1, D)))(ye, idx)


def setup_inputs(seed: int = 0) -> dict:
    key = jax.random.key(seed)
    ks = jax.random.split(key, 32)
    L, D, W, E, F = DEPTH, D_MODEL, RWKV_WIDTH, N_EXPERTS, EXPERT_FF

    def nrm(k, shape, std):
        return std * jax.random.normal(k, shape, jnp.float32)

    conv_base = jnp.array([0.25, 1.0, 0.25], jnp.float32)[None, :, None]
    return {
        'x': nrm(ks[0], (BATCH, SEQ, D), 1.0),
        'c': nrm(ks[1], (BATCH, D), 1.0),
        'ctx': nrm(ks[2], (BATCH, CTX_LEN, D), 1.0),
        'c_ctx': nrm(ks[3], (D,), 1.0),
        'w_mod': nrm(ks[4], (L, D, 6 * D), 0.5 * D ** -0.5),
        'b_mod': nrm(ks[5], (L, 6 * D), 0.02),
        'norm_mix': 1.0 + nrm(ks[6], (L, D), 0.02),
        'norm_ffn': 1.0 + nrm(ks[7], (L, D), 0.02),
        'w_in': nrm(ks[8], (L, D, IN_COLS), D ** -0.5),
        'conv_w': conv_base + nrm(ks[9], (L, SHORT_CONV, RWKV_COLS), 0.05),
        'decay_w0': -1.0 + nrm(ks[10], (L, N_DIRS, W), 0.5),
        'decay_w2': nrm(ks[11], (L, N_DIRS, LORA_DECAY, W), 0.5 * LORA_DECAY ** -0.5),
        'iclr_a0': nrm(ks[12], (L, N_DIRS, W), 0.5),
        'iclr_a2': nrm(ks[13], (L, N_DIRS, LORA_ICLR, W), 0.5 * LORA_ICLR ** -0.5),
        'gate_g2': nrm(ks[14], (L, LORA_GATE, W), LORA_GATE ** -0.5),
        'key_k': 0.85 + nrm(ks[15], (L, W), 0.05),
        'key_a': 1.0 + nrm(ks[16], (L, W), 0.05),
        'bonus_rho': nrm(ks[17], (L, RWKV_HEADS, HEAD_DIM), 0.1),
        'gn_g': 1.0 + nrm(ks[18], (L, W), 0.02),
        'gn_b': nrm(ks[19], (L, W), 0.02),
        'q_norm': 1.0 + nrm(ks[20], (L, HEAD_DIM), 0.02),
        'k_norm': 1.0 + nrm(ks[21], (L, HEAD_DIM), 0.02),
        'w_branch_rwkv': nrm(ks[22], (L, W, D), W ** -0.5),
        'w_branch_attn': nrm(ks[23], (L, ATTN_WIDTH, D), ATTN_WIDTH ** -0.5),
        'w_out': nrm(ks[24], (L, D, D), D ** -0.5),
        'w_router': nrm(ks[25], (L, D, E), D ** -0.5),
        'w_gate': nrm(ks[26], (L, E, D, F), D ** -0.5),
        'w_up': nrm(ks[27], (L, E, D, F), D ** -0.5),
        'w_down': nrm(ks[28], (L, E, F, D), F ** -0.5),
        'norm_final': 1.0 + nrm(ks[29], (D,), 0.02),
    }


def reference(x, c, ctx, c_ctx, w_mod, b_mod, norm_mix, norm_ffn, w_in, conv_w, decay_w0, decay_w2,
              iclr_a0, iclr_a2, gate_g2, key_k, key_a, bonus_rho, gn_g, gn_b, q_norm, k_norm,
              w_branch_rwkv, w_branch_attn, w_out, w_router, w_gate, w_up, w_down, norm_final):
    cos, sin = axial_rope_tables(x.shape[1])
    xl, xc = x, ctx
    for l in range(DEPTH):
        need_ctx = l < DEPTH - 1
        sh1l, sc1l, gt1l, sh2l, sc2l, gt2l = jnp.split((jax.nn.silu(c) @ w_mod[l] + b_mod[l])[:, None, :], 6, axis=-1)
        sh1c, sc1c, gt1c, sh2c, sc2c, gt2c = jnp.split(jax.nn.silu(c_ctx) @ w_mod[l] + b_mod[l], 6, axis=-1)
        hl = modulate(rms_norm(xl, norm_mix[l]), sh1l, sc1l)
        hc = modulate(rms_norm(xc, norm_mix[l]), sh1c, sc1c)
        yl, yc = token_mixer(hl @ w_in[l], hc @ w_in[l], cos, sin, conv_w[l], decay_w0[l], decay_w2[l],
                             iclr_a0[l], iclr_a2[l], gate_g2[l], key_k[l], key_a[l], bonus_rho[l],
                             gn_g[l], gn_b[l], q_norm[l], k_norm[l], w_branch_rwkv[l], w_branch_attn[l],
                             w_out[l], need_ctx)
        xl = xl + gt1l * yl
        hl2 = modulate(rms_norm(xl, norm_ffn[l]), sh2l, sc2l)
        xl = xl + gt2l * expert_choice_ffn(hl2, w_router[l], w_gate[l], w_up[l], w_down[l])
        if need_ctx:
            xc = xc + gt1c * yc
            hc2 = modulate(rms_norm(xc, norm_ffn[l]), sh2c, sc2c)
            xc = xc + gt2c * expert_choice_ffn(hc2, w_router[l], w_gate[l], w_up[l], w_down[l])
    return rms_norm(xl, norm_final)
```

```python
import functools
import math

import jax
import jax.numpy as jnp
from jax import lax
from jax.experimental import pallas as pl
from jax.experimental.pallas import tpu as pltpu

F32 = jnp.float32
BF16 = jnp.bfloat16

HEAD_DIM = 64
RWKV_HEADS = 8
RWKV_WIDTH = RWKV_HEADS * HEAD_DIM
LORA_DECAY = 64
LORA_ICLR = 64
LORA_GATE = 160
DECAY_SCALE = math.exp(-0.5)
GN_EPS = 64e-5
Q_HEADS = 8
KV_HEADS = 2
Q_PER_KV = Q_HEADS // KV_HEADS
ATTN_WIDTH = Q_HEADS * HEAD_DIM
KV_WIDTH = KV_HEADS * HEAD_DIM
ATTN_SCALE = HEAD_DIM ** -0.5
GRID_W = 64
ROPE_BASE = 10000.0
ROPE_PAIRS = HEAD_DIM // 4
N_EXPERTS = 16
EC_CAPACITY = 2
NORM_EPS = 1e-6

VMEM_LIMIT_BYTES = 56 * 1024 * 1024
LANES = 128
SUBLANES = 8

GATE_OFF = 0
GATE_COLS = 2048
RW_OFF = 2048
RW_COLS = 2048
RW_GLO = 1536
RW_DLO = 1792
RW_ALO = 1920
Q_OFF = 4096
KV_OFF = 4608
PACKED_COLS = 4864

SCAN_CHUNK = 64
ROW_TILE = 256
N_SLOTS = 11


def _cparams(*sem):
    return pltpu.CompilerParams(dimension_semantics=sem, vmem_limit_bytes=VMEM_LIMIT_BYTES)


def _split2(a):
    hi = a.astype(BF16)
    lo = (a - hi.astype(F32)).astype(BF16)
    return hi, lo


def _dot(a, b):
    return jnp.dot(a, b, preferred_element_type=F32)


def _dot_nt(a, b):
    return lax.dot_general(a, b, (((1,), (1,)), ((), ())), preferred_element_type=F32)


def _dot_tn(a, b):
    return lax.dot_general(a, b, (((0,), (0,)), ((), ())), preferred_element_type=F32)


def _dot3(a, b_hi, b_lo, dot=_dot):
    a_hi, a_lo = _split2(a)
    return dot(a_hi, b_hi) + dot(a_hi, b_lo) + dot(a_lo, b_hi)


def _dot3f(a, b, dot=_dot):
    b_hi, b_lo = _split2(b)
    return _dot3(a, b_hi, b_lo, dot)


def _segsum(a, ones_bf16):
    a_hi, a_lo = _split2(a)
    return _dot(a_hi, ones_bf16) + _dot(a_lo, ones_bf16)


def _sigmoid(x):
    return 1.0 / (1.0 + jnp.exp(-x))


def _select_mod(mod_ref, k, is_ctx):
    return jnp.where(is_ctx, mod_ref[0, k:k + 1, :], mod_ref[1, k:k + 1, :])


def _mods_kernel(c_ref, w_ref, b_ref, o_ref):
    cv = c_ref[...]
    s = cv * _sigmoid(cv)
    o_ref[...] = _dot3f(s, w_ref[...]) + b_ref[...]


def _mods(cvecs, w_mod, b_mod):
    D, N = w_mod.shape
    tn = N // 4
    return pl.pallas_call(
        _mods_kernel,
        grid=(N // tn,),
        in_specs=[pl.BlockSpec((SUBLANES, D), lambda j: (0, 0)),
                  pl.BlockSpec((D, tn), lambda j: (0, j)),
                  pl.BlockSpec((1, tn), lambda j: (0, j))],
        out_specs=pl.BlockSpec((SUBLANES, tn), lambda j: (0, j)),
        out_shape=jax.ShapeDtypeStruct((SUBLANES, N), F32),
        compiler_params=_cparams("arbitrary"),
    )(cvecs, w_mod, b_mod.reshape(1, N))


def _inproj_kernel(ctx_len, x_ref, mod_ref, nw_ref, w_ref, o_ref):
    tm = x_ref.shape[0]
    row = pl.program_id(2) * tm + lax.broadcasted_iota(jnp.int32, (tm, 1), 0)
    is_ctx = row < ctx_len
    x = x_ref[...]
    h = x * lax.rsqrt(jnp.mean(x * x, axis=-1, keepdims=True) + NORM_EPS) * nw_ref[...]
    h = h * (1.0 + _select_mod(mod_ref, 1, is_ctx)) + _select_mod(mod_ref, 0, is_ctx)
    o_ref[...] = _dot(h.astype(BF16), w_ref[...])


def _inproj(xs, mods, norm_w, w_packed, ctx_len, tm):
    B, Tc, D = xs.shape
    N = w_packed.shape[1]
    tn = N // 2
    return pl.pallas_call(
        functools.partial(_inproj_kernel, ctx_len),
        grid=(N // tn, B, Tc // tm),
        in_specs=[pl.BlockSpec((None, tm, D), lambda n, b, i: (b, i, 0)),
                  pl.BlockSpec((None, 2, SUBLANES, D), lambda n, b, i: (b, 0, 0, 0)),
                  pl.BlockSpec((1, D), lambda n, b, i: (0, 0)),
                  pl.BlockSpec((D, tn), lambda n, b, i: (0, n))],
        out_specs=pl.BlockSpec((None, tm, tn), lambda n, b, i: (b, i, n)),
        out_shape=jax.ShapeDtypeStruct((B, Tc, N), F32),
        compiler_params=_cparams("arbitrary", "arbitrary", "arbitrary"),
    )(xs, mods, norm_w.reshape(1, D), w_packed)


def _rwkv_prep_kernel(ctx_len, total_len, u_ref, up_ref, un_ref, cw_ref, vec_ref,
                      g2h_ref, g2l_ref, dwh_ref, dwl_ref, iah_ref, ial_ref, ones_ref,
                      p_ref, pc_ref):
    tm = u_ref.shape[0]
    C = SCAN_CHUNK
    W = RWKV_WIDTH
    t0 = pl.program_id(1) * tm
    prev_ok = jnp.logical_and(t0 != 0, t0 != ctx_len).astype(F32)
    next_ok = jnp.logical_and(t0 + tm != ctx_len, t0 + tm != total_len).astype(F32)
    row = lax.broadcasted_iota(jnp.int32, (tm, 1), 0)

    u = u_ref[...]
    prev_row = up_ref[SUBLANES - 1:SUBLANES, :] * prev_ok
    next_row = un_ref[0:1, :] * next_ok
    up = jnp.where(row == 0, prev_row, pltpu.roll(u, 1, 0))
    dn = jnp.where(row == tm - 1, next_row, pltpu.roll(u, tm - 1, 0))
    uc = up * cw_ref[0:1, :] + u * cw_ref[1:2, :] + dn * cw_ref[2:3, :]

    r = uc[:, 0:W]
    k = uc[:, W:2 * W]
    v = uc[:, 2 * W:3 * W]
    ones = ones_ref[...]
    key_k = vec_ref[4:5, :]
    key_a = vec_ref[5:6, :]
    rho = vec_ref[6:7, :]

    g = _dot3(_sigmoid(uc[:, RW_GLO:RW_GLO + 256]), g2h_ref[...], g2l_ref[...])
    dl = jnp.tanh(uc[:, RW_DLO:RW_DLO + 128])
    al = uc[:, RW_ALO:RW_ALO + 128]

    kk = k * key_k
    kk = kk * lax.rsqrt(_segsum(kk * kk, ones) + 1e-12)

    ti = lax.broadcasted_iota(jnp.int32, (tm, tm), 0)
    si = lax.broadcasted_iota(jnp.int32, (tm, tm), 1)
    same_chunk = (ti // C) == (si // C)
    tri = (jnp.logical_and(same_chunk, si <= ti).astype(BF16),
           jnp.logical_and(same_chunk, si >= ti).astype(BF16))

    streams = [v, g, None]
    krep_sum = None
    for z in range(2):
        d = vec_ref[z:z + 1, :] + _dot3(dl, dwh_ref[z], dwl_ref[z])
        logw = -DECAY_SCALE * _sigmoid(d)
        a = _sigmoid(vec_ref[2 + z:3 + z, :] + _dot3(al, iah_ref[z], ial_ref[z]))
        krep = k * (1.0 + (a - 1.0) * key_a)
        ka = kk * a
        krep_sum = krep if krep_sum is None else krep_sum + krep
        l_hi = logw.astype(BF16)
        rem = logw - l_hi.astype(F32)
        l_mid = rem.astype(BF16)
        l_lo = (rem - l_mid.astype(F32)).astype(BF16)
        cum = _dot(tri[z], l_hi) + _dot(tri[z], l_mid) + _dot(tri[z], l_lo)
        e_neg = jnp.exp(-cum)
        streams += [kk * jnp.exp(cum - logw), ka * e_neg, krep * e_neg, r * jnp.exp(cum)]
        for j in range(tm // C):
            last = j * C + (C - 1 if z == 0 else 0)
            pc = jnp.exp(cum[last:last + 1, :])
            for h in range(RWKV_HEADS):
                pc_ref[z, j, h:h + 1, :] = pc[:, h * HEAD_DIM:(h + 1) * HEAD_DIM]
    bonus = _segsum(r * krep_sum * rho, ones)
    streams[2] = bonus * v
    for s, val in enumerate(streams):
        for h in range(RWKV_HEADS):
            p_ref[s, h] = val[:, h * HEAD_DIM:(h + 1) * HEAD_DIM]


def _rwkv_prep(U, conv_packed, vecs, g2, dw2, ia2, ones, ctx_len, tm):
    B, Tc, _ = U.shape
    C = SCAN_CHUNK
    H = RWKV_HEADS
    rb = RW_OFF // RW_COLS
    nt = Tc // tm
    hb = tm // SUBLANES
    full = lambda a: pl.BlockSpec(a.shape, lambda b, i: (0,) * a.ndim)
    args = (conv_packed, vecs, g2[0], g2[1], dw2[0], dw2[1], ia2[0], ia2[1], ones)
    return pl.pallas_call(
        functools.partial(_rwkv_prep_kernel, ctx_len, Tc),
        grid=(B, nt),
        in_specs=[pl.BlockSpec((None, tm, RW_COLS), lambda b, i: (b, i, rb)),
                  pl.BlockSpec((None, SUBLANES, RW_COLS),
                               lambda b, i: (b, jnp.maximum(i * hb - 1, 0), rb)),
                  pl.BlockSpec((None, SUBLANES, RW_COLS),
                               lambda b, i: (b, jnp.minimum((i + 1) * hb, Tc // SUBLANES - 1), rb))]
                 + [full(a) for a in args],
        out_specs=[pl.BlockSpec((None, N_SLOTS, H, tm, HEAD_DIM), lambda b, i: (b, 0, 0, i, 0)),
                   pl.BlockSpec((None, 2, tm // C, H, HEAD_DIM), lambda b, i: (b, 0, i, 0, 0))],
        out_shape=[jax.ShapeDtypeStruct((B, N_SLOTS, H, Tc, HEAD_DIM), F32),
                   jax.ShapeDtypeStruct((B, 2, Tc // C, H, HEAD_DIM), F32)],
        compiler_params=_cparams("arbitrary", "arbitrary"),
    )(U, U, U, *args)


def _scan_kernel(v_ref, a_ref, b_ref, k_ref, r_ref, pc_ref, y_ref, state_ref):
    C = SCAN_CHUNK
    z = pl.program_id(1)

    @pl.when(pl.program_id(2) == 0)
    def _():
        state_ref[...] = jnp.zeros_like(state_ref)

    rev = z == 1
    t2 = lax.broadcasted_iota(jnp.int32, (C, 2 * C), 0)
    s2 = lax.broadcasted_iota(jnp.int32, (C, 2 * C), 1)
    right = s2 >= C
    s2 = jnp.where(right, s2 - C, s2)
    d2 = jnp.where(rev, t2 - s2, s2 - t2)
    strict_right = jnp.logical_and(d2 < 0, right)
    incl2 = d2 <= 0
    t1 = lax.broadcasted_iota(jnp.int32, (C, C), 0)
    s1 = lax.broadcasted_iota(jnp.int32, (C, C), 1)
    strict1 = jnp.where(rev, t1 - s1, s1 - t1) < 0
    ident = (t1 == s1).astype(F32)

    for h in range(RWKV_HEADS):
        A, Bt, Kt, Rt, V = a_ref[h], b_ref[h], k_ref[h], r_ref[h], v_ref[h]
        pc = pc_ref[h:h + 1, :]
        S0 = state_ref[h]
        BK = jnp.concatenate([Bt, Kt], axis=0)
        AR = jnp.concatenate([A, Rt], axis=0)
        bk_hi, bk_lo = _split2(BK)
        G = _dot3(AR, bk_hi, bk_lo, _dot_nt)
        top, bot = G[:C], G[C:]
        n_ab = jnp.where(strict1, top[:, :C], 0.0)
        n_ak_wide = jnp.where(strict_right, top, 0.0)
        n_r_wide = jnp.where(incl2, jnp.where(right, bot, -bot), 0.0)

        inv = ident - n_ab
        power = n_ab
        for _ in range(5):
            pb = power.astype(BF16)
            power = _dot(pb, pb)
            inv = _dot(inv.astype(BF16), (ident + power).astype(BF16))

        s_hi, s_lo = _split2(S0)
        ars = _dot3(AR, s_hi, s_lo, _dot_nt)
        VV = jnp.concatenate([V, V], axis=0)
        rhs = ars[:C] + _dot3f(n_ak_wide, VV)
        U = _dot3f(inv, rhs)
        UV = jnp.concatenate([U, V], axis=0)
        y_ref[h] = ars[C:] + _dot3f(n_r_wide, UV)
        W = jnp.concatenate([-U, V], axis=0)
        state_ref[h] = (S0 + _dot3(W, bk_hi, bk_lo, _dot_tn)) * pc


def _scan(P, PC, ctx_len):
    B, _, H, Tc, _ = P.shape
    C = SCAN_CHUNK
    nch = Tc // C
    ncc = ctx_len // C

    def chunk(z, j):
        rev_j = jnp.where(j < ncc, ncc - 1 - j, nch + ncc - 1 - j)
        return jnp.where(z == 1, rev_j, j)

    def slot_spec(q):
        return pl.BlockSpec((None, None, H, C, HEAD_DIM),
                            lambda b, z, j: (b, 3 + 4 * z + q, 0, chunk(z, j), 0))

    return pl.pallas_call(
        _scan_kernel,
        grid=(B, 2, nch),
        in_specs=[pl.BlockSpec((None, None, H, C, HEAD_DIM), lambda b, z, j: (b, 0, 0, chunk(z, j), 0)),
                  slot_spec(0), slot_spec(1), slot_spec(2), slot_spec(3),
                  pl.BlockSpec((None, None, None, H, HEAD_DIM), lambda b, z, j: (b, z, chunk(z, j), 0, 0))],
        out_specs=pl.BlockSpec((None, None, H, C, HEAD_DIM), lambda b, z, j: (z, b, 0, chunk(z, j), 0)),
        out_shape=jax.ShapeDtypeStruct((2, B, H, Tc, HEAD_DIM), F32),
        scratch_shapes=[pltpu.VMEM((H, HEAD_DIM, HEAD_DIM), F32)],
        compiler_params=_cparams("arbitrary", "arbitrary", "arbitrary"),
    )(P, P, P, P, P, PC)


def _attn_prep_kernel(uq_ref, ukv_ref, cos_ref, sin_ref, qn_ref, kn_ref, ones_ref, q_ref, k_ref, v_ref):
    ones = ones_ref[...]
    cos = cos_ref[...]
    sin = sin_ref[...]
    lane = lax.broadcasted_iota(jnp.int32, (1, ATTN_WIDTH), 1)
    first_half = (lane % (2 * ROPE_PAIRS)) < ROPE_PAIRS

    def norm_rope(x, gain, width):
        ms = _segsum(x * x, ones[:width, :width]) * (1.0 / HEAD_DIM)
        y = x * lax.rsqrt(ms + NORM_EPS) * gain
        partner = jnp.where(first_half[:, :width],
                            pltpu.roll(y, width - ROPE_PAIRS, 1), pltpu.roll(y, ROPE_PAIRS, 1))
        return y * cos[:, :width] + partner * sin[:, :width]

    q = norm_rope(uq_ref[...], qn_ref[...], ATTN_WIDTH) * ATTN_SCALE
    ukv = ukv_ref[...]
    k = norm_rope(ukv[:, :KV_WIDTH], kn_ref[:, :KV_WIDTH], KV_WIDTH)
    v = ukv[:, KV_WIDTH:]
    for g in range(KV_HEADS):
        for j in range(Q_PER_KV):
            o = (g * Q_PER_KV + j) * HEAD_DIM
            q_ref[g, j] = q[:, o:o + HEAD_DIM].astype(BF16)
        k_ref[g] = k[:, g * HEAD_DIM:(g + 1) * HEAD_DIM].astype(BF16)
        v_ref[g] = v[:, g * HEAD_DIM:(g + 1) * HEAD_DIM].astype(BF16)


def _attn_prep(U, cos, sin, qn, kn, ones, tm):
    B, Tc, _ = U.shape
    full = lambda a: pl.BlockSpec(a.shape, lambda b, i: (0,) * a.ndim)
    return pl.pallas_call(
        _attn_prep_kernel,
        grid=(B, Tc // tm),
        in_specs=[pl.BlockSpec((None, tm, ATTN_WIDTH), lambda b, i: (b, i, Q_OFF // ATTN_WIDTH)),
                  pl.BlockSpec((None, tm, 2 * KV_WIDTH), lambda b, i: (b, i, KV_OFF // (2 * KV_WIDTH))),
                  pl.BlockSpec((tm, ATTN_WIDTH), lambda b, i: (i, 0)),
                  pl.BlockSpec((tm, ATTN_WIDTH), lambda b, i: (i, 0)),
                  full(qn), full(kn), full(ones)],
        out_specs=[pl.BlockSpec((None, KV_HEADS, Q_PER_KV, tm, HEAD_DIM), lambda b, i: (b, 0, 0, i, 0)),
                   pl.BlockSpec((None, KV_HEADS, tm, HEAD_DIM), lambda b, i: (b, 0, i, 0)),
                   pl.BlockSpec((None, KV_HEADS, tm, HEAD_DIM), lambda b, i: (b, 0, i, 0))],
        out_shape=[jax.ShapeDtypeStruct((B, KV_HEADS, Q_PER_KV, Tc, HEAD_DIM), BF16),
                   jax.ShapeDtypeStruct((B, KV_HEADS, Tc, HEAD_DIM), BF16),
                   jax.ShapeDtypeStruct((B, KV_HEADS, Tc, HEAD_DIM), BF16)],
        compiler_params=_cparams("arbitrary", "arbitrary"),
    )(U, U, cos, sin, qn, kn, ones)


def _flash_kernel(ctx_len, q_ref, k_ref, v_ref, o_ref, m_ref, l_ref, acc_ref):
    G, tq, _ = q_ref.shape
    tk = k_ref.shape[0]
    qi = pl.program_id(2)
    ki = pl.program_id(3)
    q_is_ctx = qi * tq < ctx_len

    @pl.when(ki == 0)
    def _():
        m_ref[...] = jnp.full_like(m_ref, -jnp.inf)
        l_ref[...] = jnp.zeros_like(l_ref)
        acc_ref[...] = jnp.zeros_like(acc_ref)

    @pl.when(jnp.logical_or(jnp.logical_not(q_is_ctx), ki * tk < ctx_len))
    def _():
        q = q_ref[...].reshape(G * tq, HEAD_DIM)
        s = _dot_nt(q, k_ref[...])
        key = ki * tk + lax.broadcasted_iota(jnp.int32, (1, tk), 1)
        limit = jnp.where(q_is_ctx, ctx_len, jnp.iinfo(jnp.int32).max)
        s = jnp.where(key < limit, s, -jnp.inf)
        m_old = m_ref[...]
        m_new = jnp.maximum(m_old, jnp.max(s, axis=-1, keepdims=True))
        alpha = jnp.exp(m_old - m_new)
        p = jnp.exp(s - m_new)
        l_ref[...] = alpha * l_ref[...] + jnp.sum(p, axis=-1, keepdims=True)
        acc_ref[...] = alpha * acc_ref[...] + _dot(p.astype(BF16), v_ref[...])
        m_ref[...] = m_new

    @pl.when(ki == pl.num_programs(3) - 1)
    def _():
        o_ref[...] = (acc_ref[...] / l_ref[...]).reshape(G, tq, HEAD_DIM)


def _flash(q, k, v, ctx_len, tq, tk):
    B, KVH, G, Tc, Dh = q.shape
    return pl.pallas_call(
        functools.partial(_flash_kernel, ctx_len),
        grid=(B, KVH, Tc // tq, Tc // tk),
        in_specs=[pl.BlockSpec((None, None, G, tq, Dh), lambda b, g, i, j: (b, g, 0, i, 0)),
                  pl.BlockSpec((None, None, tk, Dh), lambda b, g, i, j: (b, g, j, 0)),
                  pl.BlockSpec((None, None, tk, Dh), lambda b, g, i, j: (b, g, j, 0))],
        out_specs=pl.BlockSpec((None, None, G, tq, Dh), lambda b, g, i, j: (b, g, 0, i, 0)),
        out_shape=jax.ShapeDtypeStruct((B, KVH, G, Tc, Dh), F32),
        scratch_shapes=[pltpu.VMEM((G * tq, 1), F32), pltpu.VMEM((G * tq, 1), F32),
                        pltpu.VMEM((G * tq, Dh), F32)],
        compiler_params=_cparams("arbitrary", "arbitrary", "arbitrary", "arbitrary"),
    )(q, k, v)


def _merge_kernel(ctx_len, x_ref, mod_ref, y_ref, g_ref, bv_ref, o_ref, gate_ref, gn_ref, nw_ref,
                  wr_ref, wa_ref, wo_ref, rth_ref, rtl_ref, x1_ref, h2_ref, aff_ref):
    tm = x_ref.shape[0]
    D = x_ref.shape[1]
    row = pl.program_id(1) * tm + lax.broadcasted_iota(jnp.int32, (tm, 1), 0)
    is_ctx = row < ctx_len

    rw = []
    for h in range(RWKV_HEADS):
        y = y_ref[0, h] + y_ref[1, h]
        mu = jnp.mean(y, axis=-1, keepdims=True)
        yc = y - mu
        var = jnp.mean(yc * yc, axis=-1, keepdims=True)
        yn = yc * lax.rsqrt(var + GN_EPS) * gn_ref[0, h:h + 1, :] + gn_ref[1, h:h + 1, :]
        rw.append((yn + bv_ref[h]) * g_ref[h])
    y_rwkv = jnp.concatenate(rw, axis=-1).astype(BF16)
    y_attn = jnp.concatenate([o_ref[g, j] for g in range(KV_HEADS) for j in range(Q_PER_KV)],
                             axis=-1).astype(BF16)
    gates = gate_ref[...]
    merged = (_sigmoid(gates[:, :D]) * _dot(y_rwkv, wr_ref[...])
              + _sigmoid(gates[:, D:]) * _dot(y_attn, wa_ref[...]))
    x1 = x_ref[...] + _select_mod(mod_ref, 2, is_ctx) * _dot(merged.astype(BF16), wo_ref[...])
    x1_ref[...] = x1

    h2 = x1 * lax.rsqrt(jnp.mean(x1 * x1, axis=-1, keepdims=True) + NORM_EPS) * nw_ref[...]
    h2 = h2 * (1.0 + _select_mod(mod_ref, 4, is_ctx)) + _select_mod(mod_ref, 3, is_ctx)
    h2_ref[...] = h2.astype(BF16)
    logits = _dot3(h2, rth_ref[...], rtl_ref[...])
    lane = lax.broadcasted_iota(jnp.int32, (1, LANES), 1)
    logits = jnp.where(lane < N_EXPERTS, logits, -jnp.inf)
    e = jnp.exp(logits - jnp.max(logits, axis=-1, keepdims=True))
    aff_ref[...] = e / jnp.sum(e, axis=-1, keepdims=True)


def _merge(xs, mods, Y, P, O, U, gn, norm_w, wr, wa, wo, rt, ctx_len, tm):
    B, Tc, D = xs.shape
    H = RWKV_HEADS
    full = lambda a: pl.BlockSpec(a.shape, lambda b, i: (0,) * a.ndim)
    nw = norm_w.reshape(1, D)
    return pl.pallas_call(
        functools.partial(_merge_kernel, ctx_len),
        grid=(B, Tc // tm),
        in_specs=[pl.BlockSpec((None, tm, D), lambda b, i: (b, i, 0)),
                  pl.BlockSpec((None, 2, SUBLANES, D), lambda b, i: (b, 0, 0, 0)),
                  pl.BlockSpec((2, None, H, tm, HEAD_DIM), lambda b, i: (0, b, 0, i, 0)),
                  pl.BlockSpec((None, None, H, tm, HEAD_DIM), lambda b, i: (b, 1, 0, i, 0)),
                  pl.BlockSpec((None, None, H, tm, HEAD_DIM), lambda b, i: (b, 2, 0, i, 0)),
                  pl.BlockSpec((None, KV_HEADS, Q_PER_KV, tm, HEAD_DIM), lambda b, i: (b, 0, 0, i, 0)),
                  pl.BlockSpec((None, tm, GATE_COLS), lambda b, i: (b, i, GATE_OFF // GATE_COLS)),
                  full(gn), full(nw), full(wr), full(wa), full(wo), full(rt[0]), full(rt[1])],
        out_specs=[pl.BlockSpec((None, tm, D), lambda b, i: (b, i, 0)),
                   pl.BlockSpec((None, tm, D), lambda b, i: (b, i, 0)),
                   pl.BlockSpec((None, tm, LANES), lambda b, i: (b, i, 0))],
        out_shape=[jax.ShapeDtypeStruct((B, Tc, D), F32),
                   jax.ShapeDtypeStruct((B, Tc, D), BF16),
                   jax.ShapeDtypeStruct((B, Tc, LANES), F32)],
        compiler_params=_cparams("arbitrary", "arbitrary"),
    )(xs, mods, Y, P, P, O, U, gn, nw, wr, wa, wo, rt[0], rt[1])


def _ffn_kernel(x_ref, gate_ref, wg_ref, wu_ref, wd_ref, o_ref, acc_ref):
    f = pl.program_id(2)

    @pl.when(f == 0)
    def _():
        acc_ref[...] = jnp.zeros_like(acc_ref)

    x = x_ref[...]
    a = _dot(x, wg_ref[...].astype(BF16))
    b = _dot(x, wu_ref[...].astype(BF16))
    hid = (a * _sigmoid(a) * b).astype(BF16)
    acc_ref[...] += _dot(hid, wd_ref[...].astype(BF16))

    @pl.when(f == pl.num_programs(2) - 1)
    def _():
        o_ref[...] = acc_ref[...] * gate_ref[...]


def _expert_ffn(xe, gate, w_gate, w_up, w_down, tm, tf):
    E, M, D = xe.shape
    F = w_gate.shape[2]
    return pl.pallas_call(
        _ffn_kernel,
        grid=(E, M // tm, F // tf),
        in_specs=[pl.BlockSpec((None, tm, D), lambda e, i, f: (e, i, 0)),
                  pl.BlockSpec((None, tm, 1), lambda e, i, f: (e, i, 0)),
                  pl.BlockSpec((None, D, tf), lambda e, i, f: (e, 0, f)),
                  pl.BlockSpec((None, D, tf), lambda e, i, f: (e, 0, f)),
                  pl.BlockSpec((None, tf, D), lambda e, i, f: (e, f, 0))],
        out_specs=pl.BlockSpec((None, tm, D), lambda e, i, f: (e, i, 0)),
        out_shape=jax.ShapeDtypeStruct((E, M, D), F32),
        scratch_shapes=[pltpu.VMEM((tm, D), F32)],
        compiler_params=_cparams("arbitrary", "arbitrary", "arbitrary"),
    )(xe, gate, w_gate, w_up, w_down)


def _moe(h2, aff, w_gate, w_up, w_down):
    B, n, D = h2.shape
    E = N_EXPERTS
    cap = EC_CAPACITY * n // E
    gate, idx = lax.top_k(jnp.swapaxes(aff, 1, 2), cap)
    xe = jax.vmap(lambda hb, ib: hb[ib])(h2, idx)
    xe = jnp.swapaxes(xe, 0, 1).reshape(E, B * cap, D)
    ge = jnp.swapaxes(gate, 0, 1).reshape(E, B * cap, 1)
    tm = min(1024, B * cap)
    ye = _expert_ffn(xe, ge, w_gate, w_up, w_down, tm, 256)
    ye = jnp.swapaxes(ye.reshape(E, B, cap, D), 0, 1)
    return jax.vmap(lambda yb, ib: jnp.zeros((n, D), yb.dtype).at[ib.reshape(-1)].add(yb.reshape(-1, D)))(ye, idx)


def _residual_kernel(ctx_len, final, x_ref, f_ref, mod_ref, nw_ref, o_ref):
    tm = x_ref.shape[0]
    row = pl.program_id(1) * tm + lax.broadcasted_iota(jnp.int32, (tm, 1), 0)
    x = x_ref[...] + _select_mod(mod_ref, 5, row < ctx_len) * f_ref[...]
    if final:
        x = x * lax.rsqrt(jnp.mean(x * x, axis=-1, keepdims=True) + NORM_EPS) * nw_ref[...]
    o_ref[...] = x


def _residual(x1, ffn, mods, norm_w, ctx_len, tm, final):
    B, Tc, D = x1.shape
    return pl.pallas_call(
        functools.partial(_residual_kernel, ctx_len, final),
        grid=(B, Tc // tm),
        in_specs=[pl.BlockSpec((None, tm, D), lambda b, i: (b, i, 0)),
                  pl.BlockSpec((None, tm, D), lambda b, i: (b, i, 0)),
                  pl.BlockSpec((None, 2, SUBLANES, D), lambda b, i: (b, 0, 0, 0)),
                  pl.BlockSpec((1, D), lambda b, i: (0, 0))],
        out_specs=pl.BlockSpec((None, tm, D), lambda b, i: (b, i, 0)),
        out_shape=jax.ShapeDtypeStruct((B, Tc, D), F32),
        compiler_params=_cparams("arbitrary", "arbitrary"),
    )(x1, ffn, mods, norm_w.reshape(1, D))


def _pack_cols(w):
    W = RWKV_WIDTH
    lead = w.shape[:-1]
    zeros = lambda n: jnp.zeros(lead + (n,), w.dtype)
    rk = 3 * W
    rw_end = rk + LORA_GATE + 2 * LORA_DECAY + 2 * LORA_ICLR
    gqa_end = rw_end + ATTN_WIDTH + 2 * KV_WIDTH
    return jnp.concatenate([
        w[..., gqa_end:],
        w[..., :rk],
        w[..., rk:rk + LORA_GATE], zeros(256 - LORA_GATE),
        w[..., rk + LORA_GATE:rw_end],
        w[..., rw_end:gqa_end],
    ], axis=-1)


def _rope_tables(n_tokens, ctx_len, width):
    rows = n_tokens // GRID_W
    row = jnp.repeat(jnp.arange(rows, dtype=F32), GRID_W)
    col = jnp.broadcast_to(jnp.arange(GRID_W, dtype=F32)[None, :], (rows, GRID_W)).reshape(-1)
    inv = jnp.power(ROPE_BASE, -jnp.arange(ROPE_PAIRS, dtype=F32) / ROPE_PAIRS)
    ang = jnp.stack([row[:, None] * inv, col[:, None] * inv], axis=1)
    cos = jnp.repeat(jnp.cos(ang)[:, :, None, :], 2, axis=2).reshape(n_tokens, HEAD_DIM)
    sin = jnp.sin(ang)
    sin = jnp.stack([-sin, sin], axis=2).reshape(n_tokens, HEAD_DIM)
    cos = jnp.concatenate([jnp.ones((ctx_len, HEAD_DIM), F32), cos], axis=0)
    sin = jnp.concatenate([jnp.zeros((ctx_len, HEAD_DIM), F32), sin], axis=0)
    reps = width // HEAD_DIM
    return jnp.tile(cos, (1, reps)), jnp.tile(sin, (1, reps))


def _pad_rows(w, rows, offset=0):
    out = jnp.zeros((rows,) + w.shape[1:], w.dtype)
    return out.at[offset:offset + w.shape[0]].set(w)


def _split_param(w):
    hi = w.astype(BF16)
    return hi, (w - hi.astype(F32)).astype(BF16)


def kernel(x, c, ctx, c_ctx, w_mod, b_mod, norm_mix, norm_ffn, w_in, conv_w, decay_w0, decay_w2, iclr_a0, iclr_a2, gate_g2, key_k, key_a, bonus_rho, gn_g, gn_b, q_norm, k_norm, w_branch_rwkv, w_branch_attn, w_out, w_router, w_gate, w_up, w_down, norm_final):
    B, T, D = x.shape
    Lc = ctx.shape[1]
    depth = w_in.shape[0]
    W = RWKV_WIDTH
    H = RWKV_HEADS
    tm = ROW_TILE
    assert Lc % tm == 0 and T % tm == 0 and tm % SCAN_CHUNK == 0 and B + 1 <= SUBLANES

    xs = jnp.concatenate([ctx, x], axis=1)
    Tc = Lc + T
    cvecs = jnp.zeros((SUBLANES, D), F32).at[:B].set(c).at[B].set(c_ctx)
    cos, sin = _rope_tables(T, Lc, ATTN_WIDTH)
    head = lax.broadcasted_iota(jnp.int32, (W, W), 0) // HEAD_DIM
    ones = (head == head.T).astype(BF16)
    tq = ROW_TILE
    tk = 3 * ROW_TILE if Tc % (3 * ROW_TILE) == 0 else ROW_TILE
    inproj_tm = 3 * ROW_TILE if Tc % (3 * ROW_TILE) == 0 else ROW_TILE

    for l in range(depth):
        m = _mods(cvecs, w_mod[l], b_mod[l]).reshape(SUBLANES, 6, D)
        m = jnp.pad(m, ((0, 0), (0, SUBLANES - 6), (0, 0)))
        mods = jnp.stack([jnp.broadcast_to(m[B], (B, SUBLANES, D)), m[:B]], axis=1)

        U = _inproj(xs, mods, norm_mix[l], _pack_cols(w_in[l]).astype(BF16), Lc, inproj_tm)

        conv_packed = _pad_rows(_pack_cols(jnp.pad(conv_w[l], ((0, 0), (0, w_in.shape[2] - conv_w.shape[2]))))
                                [:, RW_OFF:RW_OFF + RW_COLS], SUBLANES)
        vecs = _pad_rows(jnp.stack([decay_w0[l, 0], decay_w0[l, 1], iclr_a0[l, 0], iclr_a0[l, 1],
                                    key_k[l], key_a[l], bonus_rho[l].reshape(W)]), SUBLANES)
        g2 = _split_param(_pad_rows(gate_g2[l], 256))
        dw2 = _split_param(jnp.stack([_pad_rows(decay_w2[l, z], 2 * LORA_DECAY, z * LORA_DECAY) for z in range(2)]))
        ia2 = _split_param(jnp.stack([_pad_rows(iclr_a2[l, z], 2 * LORA_ICLR, z * LORA_ICLR) for z in range(2)]))
        P, PC = _rwkv_prep(U, conv_packed, vecs, g2, dw2, ia2, ones, Lc, tm)
        Y = _scan(P, PC, Lc)

        qn = jnp.tile(q_norm[l], ATTN_WIDTH // HEAD_DIM).reshape(1, ATTN_WIDTH)
        kn = jnp.tile(k_norm[l], ATTN_WIDTH // HEAD_DIM).reshape(1, ATTN_WIDTH)
        q, k, v = _attn_prep(U, cos, sin, qn, kn, ones, tm)
        O = _flash(q, k, v, Lc, tq, tk)

        gn = jnp.stack([gn_g[l].reshape(H, HEAD_DIM), gn_b[l].reshape(H, HEAD_DIM)])
        rt = _split_param(jnp.pad(w_router[l], ((0, 0), (0, LANES - N_EXPERTS))))
        x1, h2, aff = _merge(xs, mods, Y, P, O, U, gn, norm_ffn[l], w_branch_rwkv[l].astype(BF16),
                             w_branch_attn[l].astype(BF16), w_out[l].astype(BF16), rt, Lc, tm)

        last = l == depth - 1
        ffn_lat = _moe(h2[:, Lc:], aff[:, Lc:, :N_EXPERTS], w_gate[l], w_up[l], w_down[l])
        if last:
            ffn_ctx = jnp.zeros((B, Lc, D), F32)
        else:
            ffn_ctx = _moe(h2[:, :Lc], aff[:, :Lc, :N_EXPERTS], w_gate[l], w_up[l], w_down[l])
        ffn = jnp.concatenate([ffn_ctx, ffn_lat], axis=1)
        xs = _residual(x1, ffn, mods, norm_final, Lc, tm, last)
    return xs[:, Lc:]
```

```python
import functools
import math

import jax
import jax.numpy as jnp
from jax import lax
from jax.experimental import pallas as pl
from jax.experimental.pallas import tpu as pltpu

F32 = jnp.float32
BF16 = jnp.bfloat16

HEAD_DIM = 64
RWKV_HEADS = 8
RWKV_WIDTH = RWKV_HEADS * HEAD_DIM
LORA_DECAY = 64
LORA_ICLR = 64
LORA_GATE = 160
DECAY_SCALE = math.exp(-0.5)
GN_EPS = 64e-5
Q_HEADS = 8
KV_HEADS = 2
Q_PER_KV = Q_HEADS // KV_HEADS
ATTN_WIDTH = Q_HEADS * HEAD_DIM
KV_WIDTH = KV_HEADS * HEAD_DIM
ATTN_SCALE = HEAD_DIM ** -0.5
GRID_W = 64
ROPE_BASE = 10000.0
ROPE_PAIRS = HEAD_DIM // 4
N_EXPERTS = 16
EC_CAPACITY = 2
NORM_EPS = 1e-6

VMEM_LIMIT_BYTES = 56 * 1024 * 1024
LANES = 128
SUBLANES = 8

GATE_OFF = 0
GATE_COLS = 2048
RW_OFF = 2048
RW_COLS = 2048
RW_GLO = 1536
RW_DLO = 1792
RW_ALO = 1920
Q_OFF = 4096
KV_OFF = 4608
PACKED_COLS = 4864

SCAN_CHUNK = 64
ROW_TILE = 256
N_SLOTS = 11


def _cparams(*sem):
    return pltpu.CompilerParams(dimension_semantics=sem, vmem_limit_bytes=VMEM_LIMIT_BYTES)


def _split2(a):
    hi = a.astype(BF16)
    lo = (a - hi.astype(F32)).astype(BF16)
    return hi, lo


def _dot(a, b):
    return jnp.dot(a, b, preferred_element_type=F32)


def _dot_nt(a, b):
    return lax.dot_general(a, b, (((1,), (1,)), ((), ())), preferred_element_type=F32)


def _dot_tn(a, b):
    return lax.dot_general(a, b, (((0,), (0,)), ((), ())), preferred_element_type=F32)


def _dot3(a, b_hi, b_lo, dot=_dot):
    a_hi, a_lo = _split2(a)
    return dot(a_hi, b_hi) + dot(a_hi, b_lo) + dot(a_lo, b_hi)


def _dot3f(a, b, dot=_dot):
    b_hi, b_lo = _split2(b)
    return _dot3(a, b_hi, b_lo, dot)


def _segsum(a, ones_bf16):
    a_hi, a_lo = _split2(a)
    return _dot(a_hi, ones_bf16) + _dot(a_lo, ones_bf16)


def _sigmoid(x):
    return 1.0 / (1.0 + jnp.exp(-x))


def _select_mod(mod_ref, k, is_ctx):
    return jnp.where(is_ctx, mod_ref[0, k:k + 1, :], mod_ref[1, k:k + 1, :])


def _mods_kernel(c_ref, w_ref, b_ref, o_ref):
    cv = c_ref[...]
    s = cv * _sigmoid(cv)
    o_ref[...] = _dot3f(s, w_ref[...]) + b_ref[...]


def _mods(cvecs, w_mod, b_mod):
    D, N = w_mod.shape
    tn = N // 4
    return pl.pallas_call(
        _mods_kernel,
        grid=(N // tn,),
        in_specs=[pl.BlockSpec((SUBLANES, D), lambda j: (0, 0)),
                  pl.BlockSpec((D, tn), lambda j: (0, j)),
                  pl.BlockSpec((1, tn), lambda j: (0, j))],
        out_specs=pl.BlockSpec((SUBLANES, tn), lambda j: (0, j)),
        out_shape=jax.ShapeDtypeStruct((SUBLANES, N), F32),
        compiler_params=_cparams("arbitrary"),
    )(cvecs, w_mod, b_mod.reshape(1, N))


def _inproj_kernel(ctx_len, x_ref, mod_ref, nw_ref, w_ref, o_ref):
    tm = x_ref.shape[0]
    row = pl.program_id(2) * tm + lax.broadcasted_iota(jnp.int32, (tm, 1), 0)
    is_ctx = row < ctx_len
    x = x_ref[...]
    h = x * lax.rsqrt(jnp.mean(x * x, axis=-1, keepdims=True) + NORM_EPS) * nw_ref[...]
    h = h * (1.0 + _select_mod(mod_ref, 1, is_ctx)) + _select_mod(mod_ref, 0, is_ctx)
    o_ref[...] = _dot(h.astype(BF16), w_ref[...])


def _inproj(xs, mods, norm_w, w_packed, ctx_len, tm):
    B, Tc, D = xs.shape
    N = w_packed.shape[1]
    tn = N // 2
    return pl.pallas_call(
        functools.partial(_inproj_kernel, ctx_len),
        grid=(N // tn, B, Tc // tm),
        in_specs=[pl.BlockSpec((None, tm, D), lambda n, b, i: (b, i, 0)),
                  pl.BlockSpec((None, 2, SUBLANES, D), lambda n, b, i: (b, 0, 0, 0)),
                  pl.BlockSpec((1, D), lambda n, b, i: (0, 0)),
                  pl.BlockSpec((D, tn), lambda n, b, i: (0, n))],
        out_specs=pl.BlockSpec((None, tm, tn), lambda n, b, i: (b, i, n)),
        out_shape=jax.ShapeDtypeStruct((B, Tc, N), F32),
        compiler_params=_cparams("arbitrary", "arbitrary", "arbitrary"),
    )(xs, mods, norm_w.reshape(1, D), w_packed)


def _rwkv_prep_kernel(ctx_len, total_len, u_ref, up_ref, un_ref, cw_ref, vec_ref,
                      g2h_ref, g2l_ref, dwh_ref, dwl_ref, iah_ref, ial_ref, ones_ref,
                      p_ref, pc_ref):
    tm = u_ref.shape[0]
    C = SCAN_CHUNK
    W = RWKV_WIDTH
    t0 = pl.program_id(1) * tm
    prev_ok = jnp.logical_and(t0 != 0, t0 != ctx_len).astype(F32)
    next_ok = jnp.logical_and(t0 + tm != ctx_len, t0 + tm != total_len).astype(F32)
    row = lax.broadcasted_iota(jnp.int32, (tm, 1), 0)

    u = u_ref[...]
    prev_row = up_ref[SUBLANES - 1:SUBLANES, :] * prev_ok
    next_row = un_ref[0:1, :] * next_ok
    up = jnp.where(row == 0, prev_row, pltpu.roll(u, 1, 0))
    dn = jnp.where(row == tm - 1, next_row, pltpu.roll(u, tm - 1, 0))
    uc = up * cw_ref[0:1, :] + u * cw_ref[1:2, :] + dn * cw_ref[2:3, :]

    r = uc[:, 0:W]
    k = uc[:, W:2 * W]
    v = uc[:, 2 * W:3 * W]
    ones = ones_ref[...]
    key_k = vec_ref[4:5, :]
    key_a = vec_ref[5:6, :]
    rho = vec_ref[6:7, :]

    g = _dot3(_sigmoid(uc[:, RW_GLO:RW_GLO + 256]), g2h_ref[...], g2l_ref[...])
    dl = jnp.tanh(uc[:, RW_DLO:RW_DLO + 128])
    al = uc[:, RW_ALO:RW_ALO + 128]

    kk = k * key_k
    kk = kk * lax.rsqrt(_segsum(kk * kk, ones) + 1e-12)

    ti = lax.broadcasted_iota(jnp.int32, (tm, tm), 0)
    si = lax.broadcasted_iota(jnp.int32, (tm, tm), 1)
    same_chunk = (ti // C) == (si // C)
    tri = (jnp.logical_and(same_chunk, si <= ti).astype(BF16),
           jnp.logical_and(same_chunk, si >= ti).astype(BF16))

    streams = [v, g, None]
    krep_sum = None
    for z in range(2):
        d = vec_ref[z:z + 1, :] + _dot3(dl, dwh_ref[z], dwl_ref[z])
        logw = -DECAY_SCALE * _sigmoid(d)
        a = _sigmoid(vec_ref[2 + z:3 + z, :] + _dot3(al, iah_ref[z], ial_ref[z]))
        krep = k * (1.0 + (a - 1.0) * key_a)
        ka = kk * a
        krep_sum = krep if krep_sum is None else krep_sum + krep
        l_hi = logw.astype(BF16)
        rem = logw - l_hi.astype(F32)
        l_mid = rem.astype(BF16)
        l_lo = (rem - l_mid.astype(F32)).astype(BF16)
        cum = _dot(tri[z], l_hi) + _dot(tri[z], l_mid) + _dot(tri[z], l_lo)
        e_neg = jnp.exp(-cum)
        streams += [kk * jnp.exp(cum - logw), ka * e_neg, krep * e_neg, r * jnp.exp(cum)]
        for j in range(tm // C):
            last = j * C + (C - 1 if z == 0 else 0)
            pc = jnp.exp(cum[last:last + 1, :])
            for h in range(RWKV_HEADS):
                pc_ref[z, j, h:h + 1, :] = pc[:, h * HEAD_DIM:(h + 1) * HEAD_DIM]
    bonus = _segsum(r * krep_sum * rho, ones)
    streams[2] = bonus * v
    for s, val in enumerate(streams):
        for h in range(RWKV_HEADS):
            p_ref[s, h] = val[:, h * HEAD_DIM:(h + 1) * HEAD_DIM]


def _rwkv_prep(U, conv_packed, vecs, g2, dw2, ia2, ones, ctx_len, tm):
    B, Tc, _ = U.shape
    C = SCAN_CHUNK
    H = RWKV_HEADS
    rb = RW_OFF // RW_COLS
    nt = Tc // tm
    hb = tm // SUBLANES
    full = lambda a: pl.BlockSpec(a.shape, lambda b, i: (0,) * a.ndim)
    args = (conv_packed, vecs, g2[0], g2[1], dw2[0], dw2[1], ia2[0], ia2[1], ones)
    return pl.pallas_call(
        functools.partial(_rwkv_prep_kernel, ctx_len, Tc),
        grid=(B, nt),
        in_specs=[pl.BlockSpec((None, tm, RW_COLS), lambda b, i: (b, i, rb)),
                  pl.BlockSpec((None, SUBLANES, RW_COLS),
                               lambda b, i: (b, jnp.maximum(i * hb - 1, 0), rb)),
                  pl.BlockSpec((None, SUBLANES, RW_COLS),
                               lambda b, i: (b, jnp.minimum((i + 1) * hb, Tc // SUBLANES - 1), rb))]
                 + [full(a) for a in args],
        out_specs=[pl.BlockSpec((None, N_SLOTS, H, tm, HEAD_DIM), lambda b, i: (b, 0, 0, i, 0)),
                   pl.BlockSpec((None, 2, tm // C, H, HEAD_DIM), lambda b, i: (b, 0, i, 0, 0))],
        out_shape=[jax.ShapeDtypeStruct((B, N_SLOTS, H, Tc, HEAD_DIM), F32),
                   jax.ShapeDtypeStruct((B, 2, Tc // C, H, HEAD_DIM), F32)],
        compiler_params=_cparams("arbitrary", "arbitrary"),
    )(U, U, U, *args)


def _scan_kernel(v_ref, a_ref, b_ref, k_ref, r_ref, pc_ref, y_ref, state_ref):
    C = SCAN_CHUNK
    z = pl.program_id(1)

    @pl.when(pl.program_id(2) == 0)
    def _():
        state_ref[...] = jnp.zeros_like(state_ref)

    rev = z == 1
    t2 = lax.broadcasted_iota(jnp.int32, (C, 2 * C), 0)
    s2 = lax.broadcasted_iota(jnp.int32, (C, 2 * C), 1)
    right = s2 >= C
    s2 = jnp.where(right, s2 - C, s2)
    d2 = jnp.where(rev, t2 - s2, s2 - t2)
    strict_right = jnp.logical_and(d2 < 0, right)
    incl2 = d2 <= 0
    t1 = lax.broadcasted_iota(jnp.int32, (C, C), 0)
    s1 = lax.broadcasted_iota(jnp.int32, (C, C), 1)
    strict1 = jnp.where(rev, t1 - s1, s1 - t1) < 0
    ident = (t1 == s1).astype(F32)

    heads = range(RWKV_HEADS)
    V = [v_ref[h] for h in heads]
    S0 = [state_ref[h] for h in heads]
    AR = [jnp.concatenate([a_ref[h], r_ref[h]], axis=0) for h in heads]
    BK = [_split2(jnp.concatenate([b_ref[h], k_ref[h]], axis=0)) for h in heads]
    G = [_dot3(AR[h], BK[h][0], BK[h][1], _dot_nt) for h in heads]
    n_ab = [jnp.where(strict1, G[h][:C, :C], 0.0) for h in heads]
    n_ak_wide = [jnp.where(strict_right, G[h][:C], 0.0) for h in heads]
    n_r_wide = [jnp.where(incl2, jnp.where(right, G[h][C:], -G[h][C:]), 0.0) for h in heads]

    inv = [ident - n_ab[h] for h in heads]
    power = n_ab
    for _ in range(5):
        pb = [power[h].astype(BF16) for h in heads]
        power = [_dot(pb[h], pb[h]) for h in heads]
        inv = [_dot(inv[h].astype(BF16), (ident + power[h]).astype(BF16)) for h in heads]

    S0s = [_split2(S0[h]) for h in heads]
    ars = [_dot3(AR[h], S0s[h][0], S0s[h][1], _dot_nt) for h in heads]
    rhs = [ars[h][:C] + _dot3f(n_ak_wide[h], jnp.concatenate([V[h], V[h]], axis=0)) for h in heads]
    U = [_dot3f(inv[h], rhs[h]) for h in heads]
    for h in heads:
        y_ref[h] = ars[h][C:] + _dot3f(n_r_wide[h], jnp.concatenate([U[h], V[h]], axis=0))
    for h in heads:
        W = jnp.concatenate([-U[h], V[h]], axis=0)
        state_ref[h] = (S0[h] + _dot3(W, BK[h][0], BK[h][1], _dot_tn)) * pc_ref[h:h + 1, :]


def _scan(P, PC, ctx_len):
    B, _, H, Tc, _ = P.shape
    C = SCAN_CHUNK
    nch = Tc // C
    ncc = ctx_len // C

    def chunk(z, j):
        rev_j = jnp.where(j < ncc, ncc - 1 - j, nch + ncc - 1 - j)
        return jnp.where(z == 1, rev_j, j)

    def slot_spec(q):
        return pl.BlockSpec((None, None, H, C, HEAD_DIM),
                            lambda b, z, j: (b, 3 + 4 * z + q, 0, chunk(z, j), 0))

    return pl.pallas_call(
        _scan_kernel,
        grid=(B, 2, nch),
        in_specs=[pl.BlockSpec((None, None, H, C, HEAD_DIM), lambda b, z, j: (b, 0, 0, chunk(z, j), 0)),
                  slot_spec(0), slot_spec(1), slot_spec(2), slot_spec(3),
                  pl.BlockSpec((None, None, None, H, HEAD_DIM), lambda b, z, j: (b, z, chunk(z, j), 0, 0))],
        out_specs=pl.BlockSpec((None, None, H, C, HEAD_DIM), lambda b, z, j: (z, b, 0, chunk(z, j), 0)),
        out_shape=jax.ShapeDtypeStruct((2, B, H, Tc, HEAD_DIM), F32),
        scratch_shapes=[pltpu.VMEM((H, HEAD_DIM, HEAD_DIM), F32)],
        compiler_params=_cparams("arbitrary", "arbitrary", "arbitrary"),
    )(P, P, P, P, P, PC)


def _attn_prep_kernel(uq_ref, ukv_ref, cos_ref, sin_ref, qn_ref, kn_ref, ones_ref, q_ref, k_ref, v_ref):
    ones = ones_ref[...]
    cos = cos_ref[...]
    sin = sin_ref[...]
    lane = lax.broadcasted_iota(jnp.int32, (1, ATTN_WIDTH), 1)
    first_half = (lane % (2 * ROPE_PAIRS)) < ROPE_PAIRS

    def norm_rope(x, gain, width):
        ms = _segsum(x * x, ones[:width, :width]) * (1.0 / HEAD_DIM)
        y = x * lax.rsqrt(ms + NORM_EPS) * gain
        partner = jnp.where(first_half[:, :width],
                            pltpu.roll(y, width - ROPE_PAIRS, 1), pltpu.roll(y, ROPE_PAIRS, 1))
        return y * cos[:, :width] + partner * sin[:, :width]

    q = norm_rope(uq_ref[...], qn_ref[...], ATTN_WIDTH) * (ATTN_SCALE * math.log2(math.e))
    ukv = ukv_ref[...]
    k = norm_rope(ukv[:, :KV_WIDTH], kn_ref[:, :KV_WIDTH], KV_WIDTH)
    v = ukv[:, KV_WIDTH:]
    low_lanes = lax.broadcasted_iota(jnp.int32, (1, KV_WIDTH), 1) < HEAD_DIM
    for g in range(KV_HEADS):
        for j in range(Q_PER_KV):
            o = (g * Q_PER_KV + j) * HEAD_DIM
            q_ref[g, j] = q[:, o:o + HEAD_DIM].astype(BF16)
        k_ref[g] = k[:, g * HEAD_DIM:(g + 1) * HEAD_DIM].astype(BF16)
        v_g = v if g == 0 else pltpu.roll(v, (KV_HEADS - g) * HEAD_DIM, 1)
        v_ref[g] = jnp.where(low_lanes, v_g, 1.0).astype(BF16)


def _attn_prep(U, cos, sin, qn, kn, ones, tm):
    B, Tc, _ = U.shape
    full = lambda a: pl.BlockSpec(a.shape, lambda b, i: (0,) * a.ndim)
    return pl.pallas_call(
        _attn_prep_kernel,
        grid=(B, Tc // tm),
        in_specs=[pl.BlockSpec((None, tm, ATTN_WIDTH), lambda b, i: (b, i, Q_OFF // ATTN_WIDTH)),
                  pl.BlockSpec((None, tm, 2 * KV_WIDTH), lambda b, i: (b, i, KV_OFF // (2 * KV_WIDTH))),
                  pl.BlockSpec((tm, ATTN_WIDTH), lambda b, i: (i, 0)),
                  pl.BlockSpec((tm, ATTN_WIDTH), lambda b, i: (i, 0)),
                  full(qn), full(kn), full(ones)],
        out_specs=[pl.BlockSpec((None, KV_HEADS, Q_PER_KV, tm, HEAD_DIM), lambda b, i: (b, 0, 0, i, 0)),
                   pl.BlockSpec((None, KV_HEADS, tm, HEAD_DIM), lambda b, i: (b, 0, i, 0)),
                   pl.BlockSpec((None, KV_HEADS, tm, 2 * HEAD_DIM), lambda b, i: (b, 0, i, 0))],
        out_shape=[jax.ShapeDtypeStruct((B, KV_HEADS, Q_PER_KV, Tc, HEAD_DIM), BF16),
                   jax.ShapeDtypeStruct((B, KV_HEADS, Tc, HEAD_DIM), BF16),
                   jax.ShapeDtypeStruct((B, KV_HEADS, Tc, 2 * HEAD_DIM), BF16)],
        compiler_params=_cparams("arbitrary", "arbitrary"),
    )(U, U, cos, sin, qn, kn, ones)


def _flash_kernel(ctx_len, q_ref, k_ref, v_ref, o_ref, m_ref, acc_ref):
    G, tq, _ = q_ref.shape
    tk = k_ref.shape[0]
    qi = pl.program_id(2)
    ki = pl.program_id(3)
    q_is_ctx = qi * tq < ctx_len

    @pl.when(ki == 0)
    def _():
        m_ref[...] = jnp.full_like(m_ref, -jnp.inf)
        acc_ref[...] = jnp.zeros_like(acc_ref)

    def step(masked):
        k = k_ref[...]
        v = v_ref[...]
        for j in range(G):
            s = _dot_nt(q_ref[j], k)
            if masked:
                key = ki * tk + lax.broadcasted_iota(jnp.int32, (1, tk), 1)
                s = jnp.where(key < ctx_len, s, -jnp.inf)
            m_old = m_ref[j]
            m_new = jnp.maximum(m_old, jnp.max(s, axis=-1, keepdims=True))
            p = jnp.exp2(s - m_new).astype(BF16)
            acc_ref[j] = jnp.exp2(m_old - m_new) * acc_ref[j] + _dot(p, v)
            m_ref[j] = m_new

    @pl.when(jnp.logical_not(q_is_ctx))
    def _():
        step(False)

    @pl.when(jnp.logical_and(q_is_ctx, ki * tk < ctx_len))
    def _():
        step(True)

    @pl.when(ki == pl.num_programs(3) - 1)
    def _():
        for j in range(G):
            acc = acc_ref[j]
            o_ref[j] = acc[:, :HEAD_DIM] / acc[:, HEAD_DIM:]


def _flash(q, k, v, ctx_len, tq, tk):
    B, KVH, G, Tc, Dh = q.shape
    return pl.pallas_call(
        functools.partial(_flash_kernel, ctx_len),
        grid=(B, KVH, Tc // tq, Tc // tk),
        in_specs=[pl.BlockSpec((None, None, G, tq, Dh), lambda b, g, i, j: (b, g, 0, i, 0)),
                  pl.BlockSpec((None, None, tk, Dh), lambda b, g, i, j: (b, g, j, 0)),
                  pl.BlockSpec((None, None, tk, 2 * Dh), lambda b, g, i, j: (b, g, j, 0))],
        out_specs=pl.BlockSpec((None, None, G, tq, Dh), lambda b, g, i, j: (b, g, 0, i, 0)),
        out_shape=jax.ShapeDtypeStruct((B, KVH, G, Tc, Dh), F32),
        scratch_shapes=[pltpu.VMEM((G, tq, 1), F32), pltpu.VMEM((G, tq, 2 * Dh), F32)],
        compiler_params=_cparams("arbitrary", "arbitrary", "arbitrary", "arbitrary"),
    )(q, k, v)


def _merge_kernel(ctx_len, x_ref, mod_ref, y_ref, g_ref, bv_ref, o_ref, gate_ref, gn_ref, nw_ref,
                  wr_ref, wa_ref, wo_ref, rth_ref, rtl_ref, rtth_ref, rttl_ref,
                  x1_ref, h2_ref, aff_ref, afft_ref):
    tm = x_ref.shape[0]
    D = x_ref.shape[1]
    row = pl.program_id(1) * tm + lax.broadcasted_iota(jnp.int32, (tm, 1), 0)
    is_ctx = row < ctx_len

    rw = []
    for h in range(RWKV_HEADS):
        y = y_ref[0, h] + y_ref[1, h]
        mu = jnp.mean(y, axis=-1, keepdims=True)
        yc = y - mu
        var = jnp.mean(yc * yc, axis=-1, keepdims=True)
        yn = yc * lax.rsqrt(var + GN_EPS) * gn_ref[0, h:h + 1, :] + gn_ref[1, h:h + 1, :]
        rw.append((yn + bv_ref[h]) * g_ref[h])
    y_rwkv = jnp.concatenate(rw, axis=-1).astype(BF16)
    y_attn = jnp.concatenate([o_ref[g, j] for g in range(KV_HEADS) for j in range(Q_PER_KV)],
                             axis=-1).astype(BF16)
    gates = gate_ref[...]
    merged = (_sigmoid(gates[:, :D]) * _dot(y_rwkv, wr_ref[...])
              + _sigmoid(gates[:, D:]) * _dot(y_attn, wa_ref[...]))
    x1 = x_ref[...] + _select_mod(mod_ref, 2, is_ctx) * _dot(merged.astype(BF16), wo_ref[...])
    x1_ref[...] = x1

    h2 = x1 * lax.rsqrt(jnp.mean(x1 * x1, axis=-1, keepdims=True) + NORM_EPS) * nw_ref[...]
    h2 = h2 * (1.0 + _select_mod(mod_ref, 4, is_ctx)) + _select_mod(mod_ref, 3, is_ctx)
    h2_ref[...] = h2
    h_hi, h_lo = _split2(h2)
    rt_hi, rt_lo = rth_ref[...], rtl_ref[...]
    logits = _dot(h_hi, rt_hi) + _dot(h_hi, rt_lo) + _dot(h_lo, rt_hi)
    lane = lax.broadcasted_iota(jnp.int32, (1, LANES), 1)
    logits = jnp.where(lane < N_EXPERTS, logits, -jnp.inf)
    e = jnp.exp(logits - jnp.max(logits, axis=-1, keepdims=True))
    aff_ref[...] = e / jnp.sum(e, axis=-1, keepdims=True)
    rtt_hi, rtt_lo = rtth_ref[...], rttl_ref[...]
    logits_t = _dot_nt(rtt_hi, h_hi) + _dot_nt(rtt_hi, h_lo) + _dot_nt(rtt_lo, h_hi)
    et = jnp.exp(logits_t - jnp.max(logits_t, axis=0, keepdims=True))
    afft_ref[...] = et / jnp.sum(et, axis=0, keepdims=True)


def _merge(xs, mods, Y, P, O, U, gn, norm_w, wr, wa, wo, rt, rtt, ctx_len, tm):
    B, Tc, D = xs.shape
    H = RWKV_HEADS
    full = lambda a: pl.BlockSpec(a.shape, lambda b, i: (0,) * a.ndim)
    nw = norm_w.reshape(1, D)
    return pl.pallas_call(
        functools.partial(_merge_kernel, ctx_len),
        grid=(B, Tc // tm),
        in_specs=[pl.BlockSpec((None, tm, D), lambda b, i: (b, i, 0)),
                  pl.BlockSpec((None, 2, SUBLANES, D), lambda b, i: (b, 0, 0, 0)),
                  pl.BlockSpec((2, None, H, tm, HEAD_DIM), lambda b, i: (0, b, 0, i, 0)),
                  pl.BlockSpec((None, None, H, tm, HEAD_DIM), lambda b, i: (b, 1, 0, i, 0)),
                  pl.BlockSpec((None, None, H, tm, HEAD_DIM), lambda b, i: (b, 2, 0, i, 0)),
                  pl.BlockSpec((None, KV_HEADS, Q_PER_KV, tm, HEAD_DIM), lambda b, i: (b, 0, 0, i, 0)),
                  pl.BlockSpec((None, tm, GATE_COLS), lambda b, i: (b, i, GATE_OFF // GATE_COLS)),
                  full(gn), full(nw), full(wr), full(wa), full(wo), full(rt[0]), full(rt[1]),
                  full(rtt[0]), full(rtt[1])],
        out_specs=[pl.BlockSpec((None, tm, D), lambda b, i: (b, i, 0)),
                   pl.BlockSpec((None, tm, D), lambda b, i: (b, i, 0)),
                   pl.BlockSpec((None, tm, LANES), lambda b, i: (b, i, 0)),
                   pl.BlockSpec((None, N_EXPERTS, tm), lambda b, i: (b, 0, i))],
        out_shape=[jax.ShapeDtypeStruct((B, Tc, D), F32),
                   jax.ShapeDtypeStruct((B, Tc, D), F32),
                   jax.ShapeDtypeStruct((B, Tc, LANES), F32),
                   jax.ShapeDtypeStruct((B, N_EXPERTS, Tc), F32)],
        compiler_params=_cparams("arbitrary", "arbitrary"),
    )(xs, mods, Y, P, P, O, U, gn, nw, wr, wa, wo, rt[0], rt[1], rtt[0], rtt[1])


def _route_kernel(start, n, cap, tile, afft_ref, idx_ref, tab_ref, bits_ref, cnt_ref, sel_ref):
    E, Tc = afft_ref.shape
    nblk = (start + n) // LANES
    tok = lax.broadcasted_iota(jnp.int32, (1, Tc), 1)
    in_set = jnp.logical_and(tok >= start, tok < start + n)
    bits_ref[...] = jnp.where(in_set, lax.bitcast_convert_type(afft_ref[...], jnp.int32), -1)

    def search(i, thr):
        cand = jnp.bitwise_or(thr, jnp.left_shift(1, 30 - i))
        count = jnp.sum(jnp.where(bits_ref[...] >= cand, 1.0, 0.0), axis=1, keepdims=True)
        return jnp.where(count >= cap, cand, thr)

    thr = lax.fori_loop(0, 31, search, jnp.zeros((E, 1), jnp.int32))
    above = jnp.sum(jnp.where(bits_ref[...] > thr, 1.0, 0.0), axis=1, keepdims=True)
    need = cap - above

    r = lax.broadcasted_iota(jnp.int32, (LANES, LANES), 0)
    c = lax.broadcasted_iota(jnp.int32, (LANES, LANES), 1)
    tri = (r <= c).astype(BF16)
    ties_before = jnp.zeros((E, 1), F32)
    taken_before = jnp.zeros((E, 1), F32)
    for j in range(nblk):
        blk = bits_ref[:, j * LANES:(j + 1) * LANES]
        tie = jnp.where(blk == thr, 1.0, 0.0)
        tie_rank = ties_before + _dot(tie.astype(BF16), tri) - tie
        sel = jnp.where(jnp.logical_or(blk > thr, jnp.logical_and(blk == thr, tie_rank < need)), 1.0, 0.0)
        cnt_ref[:, j * LANES:(j + 1) * LANES] = taken_before + _dot(sel.astype(BF16), tri)
        sel_ref[:, j * LANES:(j + 1) * LANES] = sel
        ties_before = ties_before + jnp.sum(tie, axis=1, keepdims=True)
        taken_before = taken_before + jnp.sum(sel, axis=1, keepdims=True)
    if nblk * LANES < Tc:
        cnt_ref[:, nblk * LANES:] = jnp.full((E, Tc - nblk * LANES), float(cap), F32)
        sel_ref[:, nblk * LANES:] = jnp.zeros((E, Tc - nblk * LANES), F32)

    t = lax.broadcasted_iota(jnp.int32, (Tc, LANES), 0)
    k = lax.broadcasted_iota(jnp.int32, (Tc, LANES), 1)
    tab_ref[...] = _dot(sel_ref[...].astype(BF16), (t < k * tile).astype(BF16)).astype(jnp.int32)

    sc = min(cap, 256)
    slot = lax.broadcasted_iota(jnp.int32, (sc, 1), 0).astype(F32)

    for e in range(E):
        for ci in range(cap // sc):
            p_col = slot + float(ci * sc)

            def per_block(j, acc, e=e, p_col=p_col):
                off = pl.multiple_of(j * LANES, LANES)
                return acc + jnp.where(cnt_ref[e:e + 1, pl.ds(off, LANES)] <= p_col, 1.0, 0.0)

            acc = lax.fori_loop(0, nblk, per_block, jnp.zeros((sc, LANES), F32))
            idx_ref[e, ci * sc:(ci + 1) * sc, :] = jnp.sum(acc, axis=1, keepdims=True).astype(jnp.int32)


def _route(afft, start, n, cap, tile):
    B, E, Tc = afft.shape
    return pl.pallas_call(
        functools.partial(_route_kernel, start, n, cap, tile),
        grid=(B,),
        in_specs=[pl.BlockSpec((None, E, Tc), lambda b: (b, 0, 0))],
        out_specs=[pl.BlockSpec((None, E, cap, 1), lambda b: (b, 0, 0, 0)),
                   pl.BlockSpec((None, E, LANES), lambda b: (b, 0, 0))],
        out_shape=[jax.ShapeDtypeStruct((B, E, cap, 1), jnp.int32),
                   jax.ShapeDtypeStruct((B, E, LANES), jnp.int32)],
        scratch_shapes=[pltpu.VMEM((E, Tc), jnp.int32), pltpu.VMEM((E, Tc), F32), pltpu.VMEM((E, Tc), F32)],
        compiler_params=_cparams("arbitrary"),
    )(afft)


def _ffn_kernel(rows_ref, h_hbm, wg_ref, wu_ref, wd_ref, o_ref, xf_ref, xb_ref, acc_ref, sem):
    f = pl.program_id(2)
    tm = xf_ref.shape[0]

    @pl.when(f == 0)
    def _():
        base = (pl.program_id(0) * pl.num_programs(1) + pl.program_id(1)) * tm

        def row_copy(r):
            return pltpu.make_async_copy(h_hbm.at[pl.ds(rows_ref[base + r], 1)], xf_ref.at[pl.ds(r, 1)], sem)

        def start(r, carry):
            row_copy(r).start()
            return carry

        def wait(r, carry):
            row_copy(r).wait()
            return carry

        lax.fori_loop(0, tm, start, 0)
        lax.fori_loop(0, tm, wait, 0)
        xb_ref[...] = xf_ref[...].astype(BF16)
        acc_ref[...] = jnp.zeros_like(acc_ref)

    x = xb_ref[...]
    a = _dot(x, wg_ref[...].astype(BF16))
    b = _dot(x, wu_ref[...].astype(BF16))
    hid = (a * _sigmoid(a) * b).astype(BF16)
    acc_ref[...] += _dot(hid, wd_ref[...].astype(BF16))

    @pl.when(f == pl.num_programs(2) - 1)
    def _():
        o_ref[...] = acc_ref[...]


def _expert_ffn(rows, h_flat, w_gate, w_up, w_down, tm, tf):
    E, D, F = w_gate.shape
    M = rows.shape[0] // E
    grid_spec = pltpu.PrefetchScalarGridSpec(
        num_scalar_prefetch=1,
        grid=(E, M // tm, F // tf),
        in_specs=[pl.BlockSpec(memory_space=pl.ANY),
                  pl.BlockSpec((None, D, tf), lambda e, i, f, rows: (e, 0, f)),
                  pl.BlockSpec((None, D, tf), lambda e, i, f, rows: (e, 0, f)),
                  pl.BlockSpec((None, tf, D), lambda e, i, f, rows: (e, f, 0))],
        out_specs=pl.BlockSpec((None, tm, D), lambda e, i, f, rows: (e, i, 0)),
        scratch_shapes=[pltpu.VMEM((tm, D), F32), pltpu.VMEM((tm, D), BF16), pltpu.VMEM((tm, D), F32),
                        pltpu.SemaphoreType.DMA(())])
    return pl.pallas_call(
        _ffn_kernel,
        grid_spec=grid_spec,
        out_shape=jax.ShapeDtypeStruct((E, M, D), F32),
        compiler_params=_cparams("arbitrary", "arbitrary", "arbitrary"),
    )(rows, h_flat, w_gate, w_up, w_down)


def _combine_kernel(set_start, cap, is_ctx, final, idx_ref, tab_ref, ye_hbm, x1_ref, aff_ref, mod_ref, nw_ref,
                    *rest):
    o_ref, stage_ref, sem = rest[-3:]
    E = N_EXPERTS
    tm = x1_ref.shape[0]
    b = pl.program_id(0)
    k = pl.program_id(1) + set_start // tm
    t0 = k * tm
    stage_ref[...] = jnp.zeros_like(stage_ref)

    def row_copy(e, s):
        tok = idx_ref[(b * E + e) * cap + s] - t0
        return pltpu.make_async_copy(ye_hbm.at[e, pl.ds(b * cap + s, 1)], stage_ref.at[e, pl.ds(tok, 1)], sem)

    bounds = [(tab_ref[(b * E + e) * LANES + k], tab_ref[(b * E + e) * LANES + k + 1]) for e in range(E)]
    for e in range(E):
        def start(s, carry, e=e):
            row_copy(e, s).start()
            return carry
        lax.fori_loop(bounds[e][0], bounds[e][1], start, 0)
    for e in range(E):
        def wait(s, carry, e=e):
            row_copy(e, s).wait()
            return carry
        lax.fori_loop(bounds[e][0], bounds[e][1], wait, 0)

    aff = aff_ref[...]
    total = stage_ref[0] * aff[:, 0:1]
    for e in range(1, E):
        total = total + stage_ref[e] * aff[:, e:e + 1]
    x = x1_ref[...] + mod_ref[0 if is_ctx else 1, 5:6, :] * total
    if final:
        x = x * lax.rsqrt(jnp.mean(x * x, axis=-1, keepdims=True) + NORM_EPS) * nw_ref[...]
    o_ref[...] = x


def _combine(ye, idx, tab, x1, aff, mods, norm_w, set_start, n, cap, tm, is_ctx, final, into=None):
    B, Tc, D = x1.shape
    E = N_EXPERTS
    off = set_start // tm
    in_specs = [pl.BlockSpec(memory_space=pl.ANY),
                pl.BlockSpec((None, tm, D), lambda b, i, *_: (b, i + off, 0)),
                pl.BlockSpec((None, tm, LANES), lambda b, i, *_: (b, i + off, 0)),
                pl.BlockSpec((None, 2, SUBLANES, D), lambda b, i, *_: (b, 0, 0, 0)),
                pl.BlockSpec((1, D), lambda b, i, *_: (0, 0))]
    args = [idx.reshape(-1), tab.reshape(-1), ye, x1, aff, mods, norm_w.reshape(1, D)]
    aliases = {}
    if into is not None:
        in_specs.append(pl.BlockSpec(memory_space=pl.ANY))
        aliases = {len(args): 0}
        args.append(into)
    if final:
        out_shape = jax.ShapeDtypeStruct((B, n, D), F32)
        out_spec = pl.BlockSpec((None, tm, D), lambda b, i, *_: (b, i, 0))
    else:
        out_shape = jax.ShapeDtypeStruct((B, Tc, D), F32)
        out_spec = pl.BlockSpec((None, tm, D), lambda b, i, *_: (b, i + off, 0))
    grid_spec = pltpu.PrefetchScalarGridSpec(
        num_scalar_prefetch=2, grid=(B, n // tm), in_specs=in_specs, out_specs=out_spec,
        scratch_shapes=[pltpu.VMEM((E, tm, D), F32), pltpu.SemaphoreType.DMA(())])
    return pl.pallas_call(
        functools.partial(_combine_kernel, set_start, cap, is_ctx, final),
        grid_spec=grid_spec,
        out_shape=out_shape,
        input_output_aliases=aliases,
        compiler_params=_cparams("arbitrary", "arbitrary"),
    )(*args)


def _moe_set(afft, h_flat, w_gate, w_up, w_down, Tc, start, n, tile):
    B = afft.shape[0]
    E = N_EXPERTS
    cap = EC_CAPACITY * n // E
    idx, tab = _route(afft, start, n, cap, tile)
    idx = idx.reshape(B, E, cap)
    rows = idx + (jnp.arange(B, dtype=jnp.int32) * Tc)[:, None, None]
    rows = jnp.swapaxes(rows, 0, 1).reshape(-1)
    ye = _expert_ffn(rows, h_flat, w_gate, w_up, w_down, min(1024, B * cap), 256)
    return ye, idx, tab, cap


def _pack_cols(w):
    W = RWKV_WIDTH
    lead = w.shape[:-1]
    zeros = lambda n: jnp.zeros(lead + (n,), w.dtype)
    rk = 3 * W
    rw_end = rk + LORA_GATE + 2 * LORA_DECAY + 2 * LORA_ICLR
    gqa_end = rw_end + ATTN_WIDTH + 2 * KV_WIDTH
    return jnp.concatenate([
        w[..., gqa_end:],
        w[..., :rk],
        w[..., rk:rk + LORA_GATE], zeros(256 - LORA_GATE),
        w[..., rk + LORA_GATE:rw_end],
        w[..., rw_end:gqa_end],
    ], axis=-1)


def _rope_tables(n_tokens, ctx_len, width):
    rows = n_tokens // GRID_W
    row = jnp.repeat(jnp.arange(rows, dtype=F32), GRID_W)
    col = jnp.broadcast_to(jnp.arange(GRID_W, dtype=F32)[None, :], (rows, GRID_W)).reshape(-1)
    inv = jnp.power(ROPE_BASE, -jnp.arange(ROPE_PAIRS, dtype=F32) / ROPE_PAIRS)
    ang = jnp.stack([row[:, None] * inv, col[:, None] * inv], axis=1)
    cos = jnp.repeat(jnp.cos(ang)[:, :, None, :], 2, axis=2).reshape(n_tokens, HEAD_DIM)
    sin = jnp.sin(ang)
    sin = jnp.stack([-sin, sin], axis=2).reshape(n_tokens, HEAD_DIM)
    cos = jnp.concatenate([jnp.ones((ctx_len, HEAD_DIM), F32), cos], axis=0)
    sin = jnp.concatenate([jnp.zeros((ctx_len, HEAD_DIM), F32), sin], axis=0)
    reps = width // HEAD_DIM
    return jnp.tile(cos, (1, reps)), jnp.tile(sin, (1, reps))


def _pad_rows(w, rows, offset=0):
    out = jnp.zeros((rows,) + w.shape[1:], w.dtype)
    return out.at[offset:offset + w.shape[0]].set(w)


def _split_param(w):
    hi = w.astype(BF16)
    return hi, (w - hi.astype(F32)).astype(BF16)


def kernel(x, c, ctx, c_ctx, w_mod, b_mod, norm_mix, norm_ffn, w_in, conv_w, decay_w0, decay_w2, iclr_a0, iclr_a2, gate_g2, key_k, key_a, bonus_rho, gn_g, gn_b, q_norm, k_norm, w_branch_rwkv, w_branch_attn, w_out, w_router, w_gate, w_up, w_down, norm_final):
    B, T, D = x.shape
    Lc = ctx.shape[1]
    depth = w_in.shape[0]
    W = RWKV_WIDTH
    H = RWKV_HEADS
    tm = ROW_TILE
    assert Lc % tm == 0 and T % tm == 0 and tm % SCAN_CHUNK == 0 and B + 1 <= SUBLANES

    xs = jnp.concatenate([ctx, x], axis=1)
    Tc = Lc + T
    cvecs = jnp.zeros((SUBLANES, D), F32).at[:B].set(c).at[B].set(c_ctx)
    cos, sin = _rope_tables(T, Lc, ATTN_WIDTH)
    head = lax.broadcasted_iota(jnp.int32, (W, W), 0) // HEAD_DIM
    ones = (head == head.T).astype(BF16)
    tq = ROW_TILE
    tk = 3 * ROW_TILE if Tc % (3 * ROW_TILE) == 0 else ROW_TILE
    inproj_tm = 3 * ROW_TILE if Tc % (3 * ROW_TILE) == 0 else ROW_TILE

    for l in range(depth):
        m = _mods(cvecs, w_mod[l], b_mod[l]).reshape(SUBLANES, 6, D)
        m = jnp.pad(m, ((0, 0), (0, SUBLANES - 6), (0, 0)))
        mods = jnp.stack([jnp.broadcast_to(m[B], (B, SUBLANES, D)), m[:B]], axis=1)

        U = _inproj(xs, mods, norm_mix[l], _pack_cols(w_in[l]).astype(BF16), Lc, inproj_tm)

        conv_packed = _pad_rows(_pack_cols(jnp.pad(conv_w[l], ((0, 0), (0, w_in.shape[2] - conv_w.shape[2]))))
                                [:, RW_OFF:RW_OFF + RW_COLS], SUBLANES)
        vecs = _pad_rows(jnp.stack([decay_w0[l, 0], decay_w0[l, 1], iclr_a0[l, 0], iclr_a0[l, 1],
                                    key_k[l], key_a[l], bonus_rho[l].reshape(W)]), SUBLANES)
        g2 = _split_param(_pad_rows(gate_g2[l], 256))
        dw2 = _split_param(jnp.stack([_pad_rows(decay_w2[l, z], 2 * LORA_DECAY, z * LORA_DECAY) for z in range(2)]))
        ia2 = _split_param(jnp.stack([_pad_rows(iclr_a2[l, z], 2 * LORA_ICLR, z * LORA_ICLR) for z in range(2)]))
        P, PC = _rwkv_prep(U, conv_packed, vecs, g2, dw2, ia2, ones, Lc, tm)
        Y = _scan(P, PC, Lc)

        qn = jnp.tile(q_norm[l], ATTN_WIDTH // HEAD_DIM).reshape(1, ATTN_WIDTH)
        kn = jnp.tile(k_norm[l], ATTN_WIDTH // HEAD_DIM).reshape(1, ATTN_WIDTH)
        q, k, v = _attn_prep(U, cos, sin, qn, kn, ones, tm)
        O = _flash(q, k, v, Lc, tq, tk)

        gn = jnp.stack([gn_g[l].reshape(H, HEAD_DIM), gn_b[l].reshape(H, HEAD_DIM)])
        rt = _split_param(jnp.pad(w_router[l], ((0, 0), (0, LANES - N_EXPERTS))))
        rtt = _split_param(w_router[l].T)
        x1, h2, aff, afft = _merge(xs, mods, Y, P, O, U, gn, norm_ffn[l], w_branch_rwkv[l].astype(BF16),
                                   w_branch_attn[l].astype(BF16), w_out[l].astype(BF16), rt, rtt, Lc, tm)

        last = l == depth - 1
        h_flat = h2.reshape(B * Tc, D)
        ye, idx, tab, cap = _moe_set(afft, h_flat, w_gate[l], w_up[l], w_down[l], Tc, Lc, T, tm)
        if last:
            xs = _combine(ye, idx, tab, x1, aff, mods, norm_final, Lc, T, cap, tm, False, True)
        else:
            xs = _combine(ye, idx, tab, x1, aff, mods, norm_final, 0, Tc, cap, tm, False, False)
            ye, idx, tab, cap = _moe_set(afft, h_flat, w_gate[l], w_up[l], w_down[l], Tc, 0, Lc, tm)
            xs = _combine(ye, idx, tab, x1, aff, mods, norm_final, 0, Lc, cap, tm, True, False, into=xs)
    return xs
```

```python
import functools
import math

import jax
import jax.numpy as jnp
from jax import lax
from jax.experimental import pallas as pl
from jax.experimental.pallas import tpu as pltpu

F32 = jnp.float32
BF16 = jnp.bfloat16

HEAD_DIM = 64
RWKV_HEADS = 8
RWKV_WIDTH = RWKV_HEADS * HEAD_DIM
LORA_DECAY = 64
LORA_ICLR = 64
LORA_GATE = 160
DECAY_SCALE = math.exp(-0.5)
GN_EPS = 64e-5
Q_HEADS = 8
KV_HEADS = 2
Q_PER_KV = Q_HEADS // KV_HEADS
ATTN_WIDTH = Q_HEADS * HEAD_DIM
KV_WIDTH = KV_HEADS * HEAD_DIM
ATTN_SCALE = HEAD_DIM ** -0.5
GRID_W = 64
ROPE_BASE = 10000.0
ROPE_PAIRS = HEAD_DIM // 4
N_EXPERTS = 16
EC_CAPACITY = 2
NORM_EPS = 1e-6

VMEM_LIMIT_BYTES = 56 * 1024 * 1024
LANES = 128
SUBLANES = 8

GATE_OFF = 0
GATE_COLS = 2048
RW_OFF = 2048
RW_COLS = 2048
RW_GLO = 1536
RW_DLO = 1792
RW_ALO = 1920
Q_OFF = 4096
KV_OFF = 4608
PACKED_COLS = 4864

SCAN_CHUNK = 64
ROW_TILE = 256
N_SLOTS = 11
FLASH_UNIT_ROWS = 256


def _cparams(*sem):
    return pltpu.CompilerParams(dimension_semantics=sem, vmem_limit_bytes=VMEM_LIMIT_BYTES)


def _split2(a):
    hi = a.astype(BF16)
    lo = (a - hi.astype(F32)).astype(BF16)
    return hi, lo


def _dot(a, b):
    return jnp.dot(a, b, preferred_element_type=F32)


def _dot_nt(a, b):
    return lax.dot_general(a, b, (((1,), (1,)), ((), ())), preferred_element_type=F32)


def _dot_tn(a, b):
    return lax.dot_general(a, b, (((0,), (0,)), ((), ())), preferred_element_type=F32)


def _dot3(a, b_hi, b_lo, dot=_dot):
    a_hi, a_lo = _split2(a)
    return dot(a_hi, b_hi) + dot(a_hi, b_lo) + dot(a_lo, b_hi)


def _dot3f(a, b, dot=_dot):
    b_hi, b_lo = _split2(b)
    return _dot3(a, b_hi, b_lo, dot)


def _segsum(a, ones_bf16):
    a_hi, a_lo = _split2(a)
    return _dot(a_hi, ones_bf16) + _dot(a_lo, ones_bf16)


def _sigmoid(x):
    return 1.0 / (1.0 + jnp.exp(-x))


def _select_mod(mod_ref, k, is_ctx):
    return jnp.where(is_ctx, mod_ref[0, k:k + 1, :], mod_ref[1, k:k + 1, :])


def _mods_kernel(c_ref, w_ref, b_ref, o_ref):
    cv = c_ref[...]
    s = cv * _sigmoid(cv)
    o_ref[...] = _dot3f(s, w_ref[...]) + b_ref[...]


def _mods(cvecs, w_mod, b_mod):
    D, N = w_mod.shape
    tn = N // 4
    return pl.pallas_call(
        _mods_kernel,
        grid=(N // tn,),
        in_specs=[pl.BlockSpec((SUBLANES, D), lambda j: (0, 0)),
                  pl.BlockSpec((D, tn), lambda j: (0, j)),
                  pl.BlockSpec((1, tn), lambda j: (0, j))],
        out_specs=pl.BlockSpec((SUBLANES, tn), lambda j: (0, j)),
        out_shape=jax.ShapeDtypeStruct((SUBLANES, N), F32),
        compiler_params=_cparams("arbitrary"),
    )(cvecs, w_mod, b_mod.reshape(1, N))


def _inproj_kernel(ctx_len, x_ref, mod_ref, nw_ref, w_ref, o_ref):
    tm = x_ref.shape[0]
    row = pl.program_id(2) * tm + lax.broadcasted_iota(jnp.int32, (tm, 1), 0)
    is_ctx = row < ctx_len
    x = x_ref[...]
    h = x * lax.rsqrt(jnp.mean(x * x, axis=-1, keepdims=True) + NORM_EPS) * nw_ref[...]
    h = h * (1.0 + _select_mod(mod_ref, 1, is_ctx)) + _select_mod(mod_ref, 0, is_ctx)
    o_ref[...] = _dot(h.astype(BF16), w_ref[...])


def _inproj(xs, mods, norm_w, w_packed, ctx_len, tm):
    B, Tc, D = xs.shape
    N = w_packed.shape[1]
    tn = N // 2
    return pl.pallas_call(
        functools.partial(_inproj_kernel, ctx_len),
        grid=(N // tn, B, Tc // tm),
        in_specs=[pl.BlockSpec((None, tm, D), lambda n, b, i: (b, i, 0)),
                  pl.BlockSpec((None, 2, SUBLANES, D), lambda n, b, i: (b, 0, 0, 0)),
                  pl.BlockSpec((1, D), lambda n, b, i: (0, 0)),
                  pl.BlockSpec((D, tn), lambda n, b, i: (0, n))],
        out_specs=pl.BlockSpec((None, tm, tn), lambda n, b, i: (b, i, n)),
        out_shape=jax.ShapeDtypeStruct((B, Tc, N), F32),
        compiler_params=_cparams("arbitrary", "arbitrary", "arbitrary"),
    )(xs, mods, norm_w.reshape(1, D), w_packed)


def _rwkv_prep_kernel(ctx_len, total_len, u_ref, up_ref, un_ref, cw_ref, vec_ref,
                      g2h_ref, g2l_ref, dwh_ref, dwl_ref, iah_ref, ial_ref, ones_ref,
                      p_ref, pc_ref):
    tm = u_ref.shape[0]
    C = SCAN_CHUNK
    W = RWKV_WIDTH
    t0 = pl.program_id(1) * tm
    prev_ok = jnp.logical_and(t0 != 0, t0 != ctx_len).astype(F32)
    next_ok = jnp.logical_and(t0 + tm != ctx_len, t0 + tm != total_len).astype(F32)
    row = lax.broadcasted_iota(jnp.int32, (tm, 1), 0)

    u = u_ref[...]
    prev_row = up_ref[SUBLANES - 1:SUBLANES, :] * prev_ok
    next_row = un_ref[0:1, :] * next_ok
    up = jnp.where(row == 0, prev_row, pltpu.roll(u, 1, 0))
    dn = jnp.where(row == tm - 1, next_row, pltpu.roll(u, tm - 1, 0))
    uc = up * cw_ref[0:1, :] + u * cw_ref[1:2, :] + dn * cw_ref[2:3, :]

    r = uc[:, 0:W]
    k = uc[:, W:2 * W]
    v = uc[:, 2 * W:3 * W]
    ones = ones_ref[...]
    key_k = vec_ref[4:5, :]
    key_a = vec_ref[5:6, :]
    rho = vec_ref[6:7, :]

    g = _dot3(_sigmoid(uc[:, RW_GLO:RW_GLO + 256]), g2h_ref[...], g2l_ref[...])
    dl = jnp.tanh(uc[:, RW_DLO:RW_DLO + 128])
    al = uc[:, RW_ALO:RW_ALO + 128]

    kk = k * key_k
    kk = kk * lax.rsqrt(_segsum(kk * kk, ones) + 1e-12)

    ti = lax.broadcasted_iota(jnp.int32, (tm, tm), 0)
    si = lax.broadcasted_iota(jnp.int32, (tm, tm), 1)
    same_chunk = (ti // C) == (si // C)
    tri = (jnp.logical_and(same_chunk, si <= ti).astype(BF16),
           jnp.logical_and(same_chunk, si >= ti).astype(BF16))

    streams = [v, g, None]
    krep_sum = None
    for z in range(2):
        d = vec_ref[z:z + 1, :] + _dot3(dl, dwh_ref[z], dwl_ref[z])
        logw = -DECAY_SCALE * _sigmoid(d)
        a = _sigmoid(vec_ref[2 + z:3 + z, :] + _dot3(al, iah_ref[z], ial_ref[z]))
        krep = k * (1.0 + (a - 1.0) * key_a)
        ka = kk * a
        krep_sum = krep if krep_sum is None else krep_sum + krep
        l_hi = logw.astype(BF16)
        rem = logw - l_hi.astype(F32)
        l_mid = rem.astype(BF16)
        l_lo = (rem - l_mid.astype(F32)).astype(BF16)
        cum = _dot(tri[z], l_hi) + _dot(tri[z], l_mid) + _dot(tri[z], l_lo)
        e_neg = jnp.exp(-cum)
        streams += [kk * jnp.exp(cum - logw), ka * e_neg, krep * e_neg, r * jnp.exp(cum)]
        for j in range(tm // C):
            last = j * C + (C - 1 if z == 0 else 0)
            pc = jnp.exp(cum[last:last + 1, :])
            for h in range(RWKV_HEADS):
                pc_ref[z, j, h:h + 1, :] = pc[:, h * HEAD_DIM:(h + 1) * HEAD_DIM]
    bonus = _segsum(r * krep_sum * rho, ones)
    streams[2] = bonus * v
    for s, val in enumerate(streams):
        for h in range(RWKV_HEADS):
            p_ref[s, h] = val[:, h * HEAD_DIM:(h + 1) * HEAD_DIM]


def _rwkv_prep(U, conv_packed, vecs, g2, dw2, ia2, ones, ctx_len, tm):
    B, Tc, _ = U.shape
    C = SCAN_CHUNK
    H = RWKV_HEADS
    rb = RW_OFF // RW_COLS
    nt = Tc // tm
    hb = tm // SUBLANES
    full = lambda a: pl.BlockSpec(a.shape, lambda b, i: (0,) * a.ndim)
    args = (conv_packed, vecs, g2[0], g2[1], dw2[0], dw2[1], ia2[0], ia2[1], ones)
    return pl.pallas_call(
        functools.partial(_rwkv_prep_kernel, ctx_len, Tc),
        grid=(B, nt),
        in_specs=[pl.BlockSpec((None, tm, RW_COLS), lambda b, i: (b, i, rb)),
                  pl.BlockSpec((None, SUBLANES, RW_COLS),
                               lambda b, i: (b, jnp.maximum(i * hb - 1, 0), rb)),
                  pl.BlockSpec((None, SUBLANES, RW_COLS),
                               lambda b, i: (b, jnp.minimum((i + 1) * hb, Tc // SUBLANES - 1), rb))]
                 + [full(a) for a in args],
        out_specs=[pl.BlockSpec((None, N_SLOTS, H, tm, HEAD_DIM), lambda b, i: (b, 0, 0, i, 0)),
                   pl.BlockSpec((None, 2, tm // C, H, HEAD_DIM), lambda b, i: (b, 0, i, 0, 0))],
        out_shape=[jax.ShapeDtypeStruct((B, N_SLOTS, H, Tc, HEAD_DIM), F32),
                   jax.ShapeDtypeStruct((B, 2, Tc // C, H, HEAD_DIM), F32)],
        compiler_params=_cparams("arbitrary", "arbitrary"),
    )(U, U, U, *args)


def _scan_kernel(v_ref, a_ref, b_ref, k_ref, r_ref, pc_ref, y_ref, state_ref):
    C = SCAN_CHUNK
    z = pl.program_id(1)

    @pl.when(pl.program_id(2) == 0)
    def _():
        state_ref[...] = jnp.zeros_like(state_ref)

    rev = z == 1
    t2 = lax.broadcasted_iota(jnp.int32, (C, 2 * C), 0)
    s2 = lax.broadcasted_iota(jnp.int32, (C, 2 * C), 1)
    right = s2 >= C
    s2 = jnp.where(right, s2 - C, s2)
    d2 = jnp.where(rev, t2 - s2, s2 - t2)
    strict_right = jnp.logical_and(d2 < 0, right)
    incl2 = d2 <= 0
    t1 = lax.broadcasted_iota(jnp.int32, (C, C), 0)
    s1 = lax.broadcasted_iota(jnp.int32, (C, C), 1)
    strict1 = jnp.where(rev, t1 - s1, s1 - t1) < 0
    ident = (t1 == s1).astype(F32)

    heads = range(RWKV_HEADS)
    V = [v_ref[h] for h in heads]
    S0 = [state_ref[h] for h in heads]
    AR = [jnp.concatenate([a_ref[h], r_ref[h]], axis=0) for h in heads]
    BK = [_split2(jnp.concatenate([b_ref[h], k_ref[h]], axis=0)) for h in heads]
    G = [_dot3(AR[h], BK[h][0], BK[h][1], _dot_nt) for h in heads]
    n_ab = [jnp.where(strict1, G[h][:C, :C], 0.0) for h in heads]
    n_ak_wide = [jnp.where(strict_right, G[h][:C], 0.0) for h in heads]
    n_r_wide = [jnp.where(incl2, jnp.where(right, G[h][C:], -G[h][C:]), 0.0) for h in heads]

    inv = [ident - n_ab[h] for h in heads]
    power = n_ab
    for _ in range(5):
        pb = [power[h].astype(BF16) for h in heads]
        power = [_dot(pb[h], pb[h]) for h in heads]
        inv = [_dot(inv[h].astype(BF16), (ident + power[h]).astype(BF16)) for h in heads]

    S0s = [_split2(S0[h]) for h in heads]
    ars = [_dot3(AR[h], S0s[h][0], S0s[h][1], _dot_nt) for h in heads]
    rhs = [ars[h][:C] + _dot3f(n_ak_wide[h], jnp.concatenate([V[h], V[h]], axis=0)) for h in heads]
    U = [_dot3f(inv[h], rhs[h]) for h in heads]
    for h in heads:
        y_ref[h] = ars[h][C:] + _dot3f(n_r_wide[h], jnp.concatenate([U[h], V[h]], axis=0))
    for h in heads:
        W = jnp.concatenate([-U[h], V[h]], axis=0)
        state_ref[h] = (S0[h] + _dot3(W, BK[h][0], BK[h][1], _dot_tn)) * pc_ref[h:h + 1, :]


def _scan(P, PC, ctx_len):
    B, _, H, Tc, _ = P.shape
    C = SCAN_CHUNK
    nch = Tc // C
    ncc = ctx_len // C

    def chunk(z, j):
        rev_j = jnp.where(j < ncc, ncc - 1 - j, nch + ncc - 1 - j)
        return jnp.where(z == 1, rev_j, j)

    def slot_spec(q):
        return pl.BlockSpec((None, None, H, C, HEAD_DIM),
                            lambda b, z, j: (b, 3 + 4 * z + q, 0, chunk(z, j), 0))

    return pl.pallas_call(
        _scan_kernel,
        grid=(B, 2, nch),
        in_specs=[pl.BlockSpec((None, None, H, C, HEAD_DIM), lambda b, z, j: (b, 0, 0, chunk(z, j), 0)),
                  slot_spec(0), slot_spec(1), slot_spec(2), slot_spec(3),
                  pl.BlockSpec((None, None, None, H, HEAD_DIM), lambda b, z, j: (b, z, chunk(z, j), 0, 0))],
        out_specs=pl.BlockSpec((None, None, H, C, HEAD_DIM), lambda b, z, j: (z, b, 0, chunk(z, j), 0)),
        out_shape=jax.ShapeDtypeStruct((2, B, H, Tc, HEAD_DIM), F32),
        scratch_shapes=[pltpu.VMEM((H, HEAD_DIM, HEAD_DIM), F32)],
        compiler_params=_cparams("arbitrary", "arbitrary", "arbitrary"),
    )(P, P, P, P, P, PC)


def _attn_prep_kernel(uq_ref, ukv_ref, cos_ref, sin_ref, qn_ref, kn_ref, ones_ref, q_ref, k_ref, v_ref):
    ones = ones_ref[...]
    cos = cos_ref[...]
    sin = sin_ref[...]
    lane = lax.broadcasted_iota(jnp.int32, (1, ATTN_WIDTH), 1)
    first_half = (lane % (2 * ROPE_PAIRS)) < ROPE_PAIRS

    def norm_rope(x, gain, width):
        ms = _segsum(x * x, ones[:width, :width]) * (1.0 / HEAD_DIM)
        y = x * lax.rsqrt(ms + NORM_EPS) * gain
        partner = jnp.where(first_half[:, :width],
                            pltpu.roll(y, width - ROPE_PAIRS, 1), pltpu.roll(y, ROPE_PAIRS, 1))
        return y * cos[:, :width] + partner * sin[:, :width]

    q = norm_rope(uq_ref[...], qn_ref[...], ATTN_WIDTH) * (ATTN_SCALE * math.log2(math.e))
    ukv = ukv_ref[...]
    k = norm_rope(ukv[:, :KV_WIDTH], kn_ref[:, :KV_WIDTH], KV_WIDTH)
    v = ukv[:, KV_WIDTH:]
    low_lanes = lax.broadcasted_iota(jnp.int32, (1, KV_WIDTH), 1) < HEAD_DIM
    for g in range(KV_HEADS):
        for j in range(Q_PER_KV):
            o = (g * Q_PER_KV + j) * HEAD_DIM
            q_ref[g, j] = q[:, o:o + HEAD_DIM].astype(BF16)
        k_ref[g] = k[:, g * HEAD_DIM:(g + 1) * HEAD_DIM].astype(BF16)
        v_g = v if g == 0 else pltpu.roll(v, (KV_HEADS - g) * HEAD_DIM, 1)
        v_ref[g] = jnp.where(low_lanes, v_g, 1.0).astype(BF16)


def _attn_prep(U, cos, sin, qn, kn, ones, tm):
    B, Tc, _ = U.shape
    full = lambda a: pl.BlockSpec(a.shape, lambda b, i: (0,) * a.ndim)
    return pl.pallas_call(
        _attn_prep_kernel,
        grid=(B, Tc // tm),
        in_specs=[pl.BlockSpec((None, tm, ATTN_WIDTH), lambda b, i: (b, i, Q_OFF // ATTN_WIDTH)),
                  pl.BlockSpec((None, tm, 2 * KV_WIDTH), lambda b, i: (b, i, KV_OFF // (2 * KV_WIDTH))),
                  pl.BlockSpec((tm, ATTN_WIDTH), lambda b, i: (i, 0)),
                  pl.BlockSpec((tm, ATTN_WIDTH), lambda b, i: (i, 0)),
                  full(qn), full(kn), full(ones)],
        out_specs=[pl.BlockSpec((None, KV_HEADS, Q_PER_KV, tm, HEAD_DIM), lambda b, i: (b, 0, 0, i, 0)),
                   pl.BlockSpec((None, KV_HEADS, tm, HEAD_DIM), lambda b, i: (b, 0, i, 0)),
                   pl.BlockSpec((None, KV_HEADS, tm, 2 * HEAD_DIM), lambda b, i: (b, 0, i, 0))],
        out_shape=[jax.ShapeDtypeStruct((B, KV_HEADS, Q_PER_KV, Tc, HEAD_DIM), BF16),
                   jax.ShapeDtypeStruct((B, KV_HEADS, Tc, HEAD_DIM), BF16),
                   jax.ShapeDtypeStruct((B, KV_HEADS, Tc, 2 * HEAD_DIM), BF16)],
        compiler_params=_cparams("arbitrary", "arbitrary"),
    )(U, U, cos, sin, qn, kn, ones)


def _flash_kernel(ctx_len, q_ref, k_ref, v_ref, o_ref, m_ref, acc_ref):
    G, tq, _ = q_ref.shape
    tk = k_ref.shape[0]
    qi = pl.program_id(2)
    ki = pl.program_id(3)
    has_ctx_rows = qi * tq < ctx_len

    @pl.when(ki == 0)
    def _():
        m_ref[...] = jnp.full_like(m_ref, -jnp.inf)
        acc_ref[...] = jnp.zeros_like(acc_ref)

    def step(masked):
        k = k_ref[...]
        v = v_ref[...]
        rows = FLASH_UNIT_ROWS
        units = [(j, r0) for j in range(G) for r0 in range(0, tq, rows)]

        def logits(u):
            j, r0 = u
            s = _dot_nt(q_ref[j, r0:r0 + rows, :], k)
            if masked:
                key = ki * tk + lax.broadcasted_iota(jnp.int32, (1, tk), 1)
                row = qi * tq + r0 + lax.broadcasted_iota(jnp.int32, (rows, 1), 0)
                s = jnp.where(jnp.logical_or(row >= ctx_len, key < ctx_len), s, -jnp.inf)
            return s

        s_next = logits(units[0])
        for n, (j, r0) in enumerate(units):
            s = s_next
            if n + 1 < len(units):
                s_next = logits(units[n + 1])
            m_old = m_ref[j, r0:r0 + rows, :]
            m_new = jnp.maximum(m_old, jnp.max(s, axis=-1, keepdims=True))
            p = jnp.exp2(s - m_new).astype(BF16)
            acc_ref[j, r0:r0 + rows, :] = jnp.exp2(m_old - m_new) * acc_ref[j, r0:r0 + rows, :] + _dot(p, v)
            m_ref[j, r0:r0 + rows, :] = m_new

    @pl.when(jnp.logical_not(has_ctx_rows))
    def _():
        step(False)

    @pl.when(has_ctx_rows)
    def _():
        step(True)

    @pl.when(ki == pl.num_programs(3) - 1)
    def _():
        for j in range(G):
            acc = acc_ref[j]
            o_ref[j] = acc[:, :HEAD_DIM] / acc[:, HEAD_DIM:]


def _flash(q, k, v, ctx_len, tq, tk):
    B, KVH, G, Tc, Dh = q.shape
    return pl.pallas_call(
        functools.partial(_flash_kernel, ctx_len),
        grid=(B, KVH, Tc // tq, Tc // tk),
        in_specs=[pl.BlockSpec((None, None, G, tq, Dh), lambda b, g, i, j: (b, g, 0, i, 0)),
                  pl.BlockSpec((None, None, tk, Dh), lambda b, g, i, j: (b, g, j, 0)),
                  pl.BlockSpec((None, None, tk, 2 * Dh), lambda b, g, i, j: (b, g, j, 0))],
        out_specs=pl.BlockSpec((None, None, G, tq, Dh), lambda b, g, i, j: (b, g, 0, i, 0)),
        out_shape=jax.ShapeDtypeStruct((B, KVH, G, Tc, Dh), F32),
        scratch_shapes=[pltpu.VMEM((G, tq, 1), F32), pltpu.VMEM((G, tq, 2 * Dh), F32)],
        compiler_params=_cparams("arbitrary", "arbitrary", "arbitrary", "arbitrary"),
    )(q, k, v)


def _merge_kernel(ctx_len, x_ref, mod_ref, y_ref, g_ref, bv_ref, o_ref, gate_ref, gn_ref, nw_ref,
                  wr_ref, wa_ref, wo_ref, rth_ref, rtl_ref, rtth_ref, rttl_ref,
                  x1_ref, h2_ref, aff_ref, afft_ref):
    tm = x_ref.shape[0]
    D = x_ref.shape[1]
    row = pl.program_id(1) * tm + lax.broadcasted_iota(jnp.int32, (tm, 1), 0)
    is_ctx = row < ctx_len

    rw = []
    for h in range(RWKV_HEADS):
        y = y_ref[0, h] + y_ref[1, h]
        mu = jnp.mean(y, axis=-1, keepdims=True)
        yc = y - mu
        var = jnp.mean(yc * yc, axis=-1, keepdims=True)
        yn = yc * lax.rsqrt(var + GN_EPS) * gn_ref[0, h:h + 1, :] + gn_ref[1, h:h + 1, :]
        rw.append((yn + bv_ref[h]) * g_ref[h])
    y_rwkv = jnp.concatenate(rw, axis=-1).astype(BF16)
    y_attn = jnp.concatenate([o_ref[g, j] for g in range(KV_HEADS) for j in range(Q_PER_KV)],
                             axis=-1).astype(BF16)
    gates = gate_ref[...]
    merged = (_sigmoid(gates[:, :D]) * _dot(y_rwkv, wr_ref[...])
              + _sigmoid(gates[:, D:]) * _dot(y_attn, wa_ref[...]))
    x1 = x_ref[...] + _select_mod(mod_ref, 2, is_ctx) * _dot(merged.astype(BF16), wo_ref[...])
    x1_ref[...] = x1

    h2 = x1 * lax.rsqrt(jnp.mean(x1 * x1, axis=-1, keepdims=True) + NORM_EPS) * nw_ref[...]
    h2 = h2 * (1.0 + _select_mod(mod_ref, 4, is_ctx)) + _select_mod(mod_ref, 3, is_ctx)
    h2_ref[...] = h2
    h_hi, h_lo = _split2(h2)
    rt_hi, rt_lo = rth_ref[...], rtl_ref[...]
    logits = _dot(h_hi, rt_hi) + _dot(h_hi, rt_lo) + _dot(h_lo, rt_hi)
    lane = lax.broadcasted_iota(jnp.int32, (1, LANES), 1)
    logits = jnp.where(lane < N_EXPERTS, logits, -jnp.inf)
    e = jnp.exp(logits - jnp.max(logits, axis=-1, keepdims=True))
    aff_ref[...] = e / jnp.sum(e, axis=-1, keepdims=True)
    rtt_hi, rtt_lo = rtth_ref[...], rttl_ref[...]
    logits_t = _dot_nt(rtt_hi, h_hi) + _dot_nt(rtt_hi, h_lo) + _dot_nt(rtt_lo, h_hi)
    et = jnp.exp(logits_t - jnp.max(logits_t, axis=0, keepdims=True))
    afft_ref[...] = et / jnp.sum(et, axis=0, keepdims=True)


def _merge(xs, mods, Y, P, O, U, gn, norm_w, wr, wa, wo, rt, rtt, ctx_len, tm):
    B, Tc, D = xs.shape
    H = RWKV_HEADS
    full = lambda a: pl.BlockSpec(a.shape, lambda b, i: (0,) * a.ndim)
    nw = norm_w.reshape(1, D)
    return pl.pallas_call(
        functools.partial(_merge_kernel, ctx_len),
        grid=(B, Tc // tm),
        in_specs=[pl.BlockSpec((None, tm, D), lambda b, i: (b, i, 0)),
                  pl.BlockSpec((None, 2, SUBLANES, D), lambda b, i: (b, 0, 0, 0)),
                  pl.BlockSpec((2, None, H, tm, HEAD_DIM), lambda b, i: (0, b, 0, i, 0)),
                  pl.BlockSpec((None, None, H, tm, HEAD_DIM), lambda b, i: (b, 1, 0, i, 0)),
                  pl.BlockSpec((None, None, H, tm, HEAD_DIM), lambda b, i: (b, 2, 0, i, 0)),
                  pl.BlockSpec((None, KV_HEADS, Q_PER_KV, tm, HEAD_DIM), lambda b, i: (b, 0, 0, i, 0)),
                  pl.BlockSpec((None, tm, GATE_COLS), lambda b, i: (b, i, GATE_OFF // GATE_COLS)),
                  full(gn), full(nw), full(wr), full(wa), full(wo), full(rt[0]), full(rt[1]),
                  full(rtt[0]), full(rtt[1])],
        out_specs=[pl.BlockSpec((None, tm, D), lambda b, i: (b, i, 0)),
                   pl.BlockSpec((None, tm, D), lambda b, i: (b, i, 0)),
                   pl.BlockSpec((None, tm, LANES), lambda b, i: (b, i, 0)),
                   pl.BlockSpec((None, N_EXPERTS, tm), lambda b, i: (b, 0, i))],
        out_shape=[jax.ShapeDtypeStruct((B, Tc, D), F32),
                   jax.ShapeDtypeStruct((B, Tc, D), F32),
                   jax.ShapeDtypeStruct((B, Tc, LANES), F32),
                   jax.ShapeDtypeStruct((B, N_EXPERTS, Tc), F32)],
        compiler_params=_cparams("arbitrary", "arbitrary"),
    )(xs, mods, Y, P, P, O, U, gn, nw, wr, wa, wo, rt[0], rt[1], rtt[0], rtt[1])


def _route_kernel(start, n, cap, tile, afft_ref, idx_ref, tab_ref, bits_ref, cnt_ref, sel_ref):
    E, Tc = afft_ref.shape
    nblk = (start + n) // LANES
    tok = lax.broadcasted_iota(jnp.int32, (1, Tc), 1)
    in_set = jnp.logical_and(tok >= start, tok < start + n)
    bits_ref[...] = jnp.where(in_set, lax.bitcast_convert_type(afft_ref[...], jnp.int32), -1)

    def search(i, thr):
        cand = jnp.bitwise_or(thr, jnp.left_shift(1, 30 - i))
        count = jnp.sum(jnp.where(bits_ref[...] >= cand, 1.0, 0.0), axis=1, keepdims=True)
        return jnp.where(count >= cap, cand, thr)

    thr = lax.fori_loop(0, 31, search, jnp.zeros((E, 1), jnp.int32))
    above = jnp.sum(jnp.where(bits_ref[...] > thr, 1.0, 0.0), axis=1, keepdims=True)
    need = cap - above

    r = lax.broadcasted_iota(jnp.int32, (LANES, LANES), 0)
    c = lax.broadcasted_iota(jnp.int32, (LANES, LANES), 1)
    tri = (r <= c).astype(BF16)
    ties_before = jnp.zeros((E, 1), F32)
    taken_before = jnp.zeros((E, 1), F32)
    for j in range(nblk):
        blk = bits_ref[:, j * LANES:(j + 1) * LANES]
        tie = jnp.where(blk == thr, 1.0, 0.0)
        tie_rank = ties_before + _dot(tie.astype(BF16), tri) - tie
        sel = jnp.where(jnp.logical_or(blk > thr, jnp.logical_and(blk == thr, tie_rank < need)), 1.0, 0.0)
        cnt_ref[:, j * LANES:(j + 1) * LANES] = taken_before + _dot(sel.astype(BF16), tri)
        sel_ref[:, j * LANES:(j + 1) * LANES] = sel
        ties_before = ties_before + jnp.sum(tie, axis=1, keepdims=True)
        taken_before = taken_before + jnp.sum(sel, axis=1, keepdims=True)
    if nblk * LANES < Tc:
        cnt_ref[:, nblk * LANES:] = jnp.full((E, Tc - nblk * LANES), float(cap), F32)
        sel_ref[:, nblk * LANES:] = jnp.zeros((E, Tc - nblk * LANES), F32)

    t = lax.broadcasted_iota(jnp.int32, (Tc, LANES), 0)
    k = lax.broadcasted_iota(jnp.int32, (Tc, LANES), 1)
    tab_ref[...] = _dot(sel_ref[...].astype(BF16), (t < k * tile).astype(BF16)).astype(jnp.int32)

    sc = min(cap, 256)
    slot = lax.broadcasted_iota(jnp.int32, (sc, 1), 0).astype(F32)

    for e in range(E):
        for ci in range(cap // sc):
            p_col = slot + float(ci * sc)

            def per_block(j, acc, e=e, p_col=p_col):
                off = pl.multiple_of(j * LANES, LANES)
                return acc + jnp.where(cnt_ref[e:e + 1, pl.ds(off, LANES)] <= p_col, 1.0, 0.0)

            acc = lax.fori_loop(0, nblk, per_block, jnp.zeros((sc, LANES), F32))
            idx_ref[e, ci * sc:(ci + 1) * sc, :] = jnp.sum(acc, axis=1, keepdims=True).astype(jnp.int32)


def _route(afft, start, n, cap, tile):
    B, E, Tc = afft.shape
    return pl.pallas_call(
        functools.partial(_route_kernel, start, n, cap, tile),
        grid=(B,),
        in_specs=[pl.BlockSpec((None, E, Tc), lambda b: (b, 0, 0))],
        out_specs=[pl.BlockSpec((None, E, cap, 1), lambda b: (b, 0, 0, 0)),
                   pl.BlockSpec((None, E, LANES), lambda b: (b, 0, 0))],
        out_shape=[jax.ShapeDtypeStruct((B, E, cap, 1), jnp.int32),
                   jax.ShapeDtypeStruct((B, E, LANES), jnp.int32)],
        scratch_shapes=[pltpu.VMEM((E, Tc), jnp.int32), pltpu.VMEM((E, Tc), F32), pltpu.VMEM((E, Tc), F32)],
        compiler_params=_cparams("arbitrary"),
    )(afft)


def _ffn_kernel(n_steps, rows_ref, h_hbm, wg_ref, wu_ref, wd_ref, o_ref, xf_ref, xb_ref, acc_ref, sem):
    f = pl.program_id(2)
    tm = xb_ref.shape[0]
    tile = pl.program_id(0) * pl.num_programs(1) + pl.program_id(1)
    last_tile = pl.num_programs(0) * pl.num_programs(1) - 1
    slot = tile % 2

    def row_copy(buf, r, src_row):
        return pltpu.make_async_copy(h_hbm.at[pl.ds(src_row, 1)], xf_ref.at[buf, pl.ds(r, 1)], sem.at[buf])

    @pl.when(jnp.logical_and(tile == 0, f == 0))
    def _():
        def start(r, carry):
            row_copy(0, r, rows_ref[r]).start()
            return carry
        lax.fori_loop(0, tm, start, 0)

    next_base = jnp.minimum(tile + 1, last_tile) * tm
    per_step = tm // n_steps

    @pl.when(f == 0)
    def _():
        for r in range(tm):
            row_copy(slot, r, 0).wait()
        xb_ref[...] = xf_ref[slot].astype(BF16)
        acc_ref[...] = jnp.zeros_like(acc_ref)
        for r in range(per_step * n_steps, tm):
            row_copy(1 - slot, r, rows_ref[next_base + r]).start()

    for i in range(per_step):
        r = i * n_steps + f
        row_copy(1 - slot, r, rows_ref[next_base + r]).start()

    x = xb_ref[...]
    a = _dot(x, wg_ref[...].astype(BF16))
    b = _dot(x, wu_ref[...].astype(BF16))
    hid = (a * _sigmoid(a) * b).astype(BF16)
    acc_ref[...] += _dot(hid, wd_ref[...].astype(BF16))

    @pl.when(f == n_steps - 1)
    def _():
        o_ref[...] = acc_ref[...]

    @pl.when(jnp.logical_and(tile == last_tile, f == n_steps - 1))
    def _():
        for r in range(tm):
            row_copy(1 - slot, r, 0).wait()


def _expert_ffn(rows, h_flat, layer, w_gate, w_up, w_down, tm, tf):
    _, E, D, F = w_gate.shape
    M = rows.shape[0] // E
    n_steps = F // tf
    grid_spec = pltpu.PrefetchScalarGridSpec(
        num_scalar_prefetch=1,
        grid=(E, M // tm, n_steps),
        in_specs=[pl.BlockSpec(memory_space=pl.ANY),
                  pl.BlockSpec((None, None, D, tf), lambda e, i, f, rows: (layer, e, 0, f)),
                  pl.BlockSpec((None, None, D, tf), lambda e, i, f, rows: (layer, e, 0, f)),
                  pl.BlockSpec((None, None, tf, D), lambda e, i, f, rows: (layer, e, f, 0))],
        out_specs=pl.BlockSpec((None, tm, D), lambda e, i, f, rows: (e, i, 0)),
        scratch_shapes=[pltpu.VMEM((2, tm, D), F32), pltpu.VMEM((tm, D), BF16), pltpu.VMEM((tm, D), F32),
                        pltpu.SemaphoreType.DMA((2,))])
    return pl.pallas_call(
        functools.partial(_ffn_kernel, n_steps),
        grid_spec=grid_spec,
        out_shape=jax.ShapeDtypeStruct((E, M, D), F32),
        compiler_params=_cparams("arbitrary", "arbitrary", "arbitrary"),
    )(rows, h_flat, w_gate, w_up, w_down)


def _combine_kernel(set_start, cap, is_ctx, final, idx_ref, tab_ref, ye_hbm, x1_ref, aff_ref, mod_ref, nw_ref,
                    *rest):
    o_ref, stage_ref, sem = rest[-3:]
    E = N_EXPERTS
    tm = x1_ref.shape[0]
    b = pl.program_id(0)
    k = pl.program_id(1) + set_start // tm
    t0 = k * tm
    stage_ref[...] = jnp.zeros_like(stage_ref)

    def row_copy(e, s):
        tok = idx_ref[(b * E + e) * cap + s] - t0
        return pltpu.make_async_copy(ye_hbm.at[e, pl.ds(b * cap + s, 1)], stage_ref.at[e, pl.ds(tok, 1)], sem)

    bounds = [(tab_ref[(b * E + e) * LANES + k], tab_ref[(b * E + e) * LANES + k + 1]) for e in range(E)]
    for e in range(E):
        def start(s, carry, e=e):
            row_copy(e, s).start()
            return carry
        lax.fori_loop(bounds[e][0], bounds[e][1], start, 0)
    for e in range(E):
        def wait(s, carry, e=e):
            row_copy(e, s).wait()
            return carry
        lax.fori_loop(bounds[e][0], bounds[e][1], wait, 0)

    aff = aff_ref[...]
    total = stage_ref[0] * aff[:, 0:1]
    for e in range(1, E):
        total = total + stage_ref[e] * aff[:, e:e + 1]
    x = x1_ref[...] + mod_ref[0 if is_ctx else 1, 5:6, :] * total
    if final:
        x = x * lax.rsqrt(jnp.mean(x * x, axis=-1, keepdims=True) + NORM_EPS) * nw_ref[...]
    o_ref[...] = x


def _combine(ye, idx, tab, x1, aff, mods, norm_w, set_start, n, cap, tm, is_ctx, final, into=None):
    B, Tc, D = x1.shape
    E = N_EXPERTS
    off = set_start // tm
    in_specs = [pl.BlockSpec(memory_space=pl.ANY),
                pl.BlockSpec((None, tm, D), lambda b, i, *_: (b, i + off, 0)),
                pl.BlockSpec((None, tm, LANES), lambda b, i, *_: (b, i + off, 0)),
                pl.BlockSpec((None, 2, SUBLANES, D), lambda b, i, *_: (b, 0, 0, 0)),
                pl.BlockSpec((1, D), lambda b, i, *_: (0, 0))]
    args = [idx.reshape(-1), tab.reshape(-1), ye, x1, aff, mods, norm_w.reshape(1, D)]
    aliases = {}
    if into is not None:
        in_specs.append(pl.BlockSpec(memory_space=pl.ANY))
        aliases = {len(args): 0}
        args.append(into)
    if final:
        out_shape = jax.ShapeDtypeStruct((B, n, D), F32)
        out_spec = pl.BlockSpec((None, tm, D), lambda b, i, *_: (b, i, 0))
    else:
        out_shape = jax.ShapeDtypeStruct((B, Tc, D), F32)
        out_spec = pl.BlockSpec((None, tm, D), lambda b, i, *_: (b, i + off, 0))
    grid_spec = pltpu.PrefetchScalarGridSpec(
        num_scalar_prefetch=2, grid=(B, n // tm), in_specs=in_specs, out_specs=out_spec,
        scratch_shapes=[pltpu.VMEM((E, tm, D), F32), pltpu.SemaphoreType.DMA(())])
    return pl.pallas_call(
        functools.partial(_combine_kernel, set_start, cap, is_ctx, final),
        grid_spec=grid_spec,
        out_shape=out_shape,
        input_output_aliases=aliases,
        compiler_params=_cparams("arbitrary", "arbitrary"),
    )(*args)


def _moe_set(afft, h_flat, layer, w_gate, w_up, w_down, Tc, start, n, tile):
    B = afft.shape[0]
    E = N_EXPERTS
    cap = EC_CAPACITY * n // E
    idx, tab = _route(afft, start, n, cap, tile)
    idx = idx.reshape(B, E, cap)
    rows = idx + (jnp.arange(B, dtype=jnp.int32) * Tc)[:, None, None]
    rows = jnp.swapaxes(rows, 0, 1).reshape(-1)
    ye = _expert_ffn(rows, h_flat, layer, w_gate, w_up, w_down, min(1024, B * cap), 256)
    return ye, idx, tab, cap


def _pack_cols(w):
    W = RWKV_WIDTH
    lead = w.shape[:-1]
    zeros = lambda n: jnp.zeros(lead + (n,), w.dtype)
    rk = 3 * W
    rw_end = rk + LORA_GATE + 2 * LORA_DECAY + 2 * LORA_ICLR
    gqa_end = rw_end + ATTN_WIDTH + 2 * KV_WIDTH
    return jnp.concatenate([
        w[..., gqa_end:],
        w[..., :rk],
        w[..., rk:rk + LORA_GATE], zeros(256 - LORA_GATE),
        w[..., rk + LORA_GATE:rw_end],
        w[..., rw_end:gqa_end],
    ], axis=-1)


def _rope_tables(n_tokens, ctx_len, width):
    rows = n_tokens // GRID_W
    row = jnp.repeat(jnp.arange(rows, dtype=F32), GRID_W)
    col = jnp.broadcast_to(jnp.arange(GRID_W, dtype=F32)[None, :], (rows, GRID_W)).reshape(-1)
    inv = jnp.power(ROPE_BASE, -jnp.arange(ROPE_PAIRS, dtype=F32) / ROPE_PAIRS)
    ang = jnp.stack([row[:, None] * inv, col[:, None] * inv], axis=1)
    cos = jnp.repeat(jnp.cos(ang)[:, :, None, :], 2, axis=2).reshape(n_tokens, HEAD_DIM)
    sin = jnp.sin(ang)
    sin = jnp.stack([-sin, sin], axis=2).reshape(n_tokens, HEAD_DIM)
    cos = jnp.concatenate([jnp.ones((ctx_len, HEAD_DIM), F32), cos], axis=0)
    sin = jnp.concatenate([jnp.zeros((ctx_len, HEAD_DIM), F32), sin], axis=0)
    reps = width // HEAD_DIM
    return jnp.tile(cos, (1, reps)), jnp.tile(sin, (1, reps))


def _pad_rows(w, rows, offset=0):
    out = jnp.zeros((rows,) + w.shape[1:], w.dtype)
    return out.at[offset:offset + w.shape[0]].set(w)


def _split_param(w):
    hi = w.astype(BF16)
    return hi, (w - hi.astype(F32)).astype(BF16)


def kernel(x, c, ctx, c_ctx, w_mod, b_mod, norm_mix, norm_ffn, w_in, conv_w, decay_w0, decay_w2, iclr_a0, iclr_a2, gate_g2, key_k, key_a, bonus_rho, gn_g, gn_b, q_norm, k_norm, w_branch_rwkv, w_branch_attn, w_out, w_router, w_gate, w_up, w_down, norm_final):
    B, T, D = x.shape
    Lc = ctx.shape[1]
    depth = w_in.shape[0]
    W = RWKV_WIDTH
    H = RWKV_HEADS
    tm = ROW_TILE
    assert Lc % tm == 0 and T % tm == 0 and tm % SCAN_CHUNK == 0 and B + 1 <= SUBLANES

    xs = jnp.concatenate([ctx, x], axis=1)
    Tc = Lc + T
    cvecs = jnp.zeros((SUBLANES, D), F32).at[:B].set(c).at[B].set(c_ctx)
    cos, sin = _rope_tables(T, Lc, ATTN_WIDTH)
    head = lax.broadcasted_iota(jnp.int32, (W, W), 0) // HEAD_DIM
    ones = (head == head.T).astype(BF16)
    tk = 3 * ROW_TILE if Tc % (3 * ROW_TILE) == 0 else ROW_TILE
    tq = tk
    inproj_tm = 3 * ROW_TILE if Tc % (3 * ROW_TILE) == 0 else ROW_TILE

    for l in range(depth):
        m = _mods(cvecs, w_mod[l], b_mod[l]).reshape(SUBLANES, 6, D)
        m = jnp.pad(m, ((0, 0), (0, SUBLANES - 6), (0, 0)))
        mods = jnp.stack([jnp.broadcast_to(m[B], (B, SUBLANES, D)), m[:B]], axis=1)

        U = _inproj(xs, mods, norm_mix[l], _pack_cols(w_in[l]).astype(BF16), Lc, inproj_tm)

        conv_packed = _pad_rows(_pack_cols(jnp.pad(conv_w[l], ((0, 0), (0, w_in.shape[2] - conv_w.shape[2]))))
                                [:, RW_OFF:RW_OFF + RW_COLS], SUBLANES)
        vecs = _pad_rows(jnp.stack([decay_w0[l, 0], decay_w0[l, 1], iclr_a0[l, 0], iclr_a0[l, 1],
                                    key_k[l], key_a[l], bonus_rho[l].reshape(W)]), SUBLANES)
        g2 = _split_param(_pad_rows(gate_g2[l], 256))
        dw2 = _split_param(jnp.stack([_pad_rows(decay_w2[l, z], 2 * LORA_DECAY, z * LORA_DECAY) for z in range(2)]))
        ia2 = _split_param(jnp.stack([_pad_rows(iclr_a2[l, z], 2 * LORA_ICLR, z * LORA_ICLR) for z in range(2)]))
        P, PC = _rwkv_prep(U, conv_packed, vecs, g2, dw2, ia2, ones, Lc, tm)
        Y = _scan(P, PC, Lc)

        qn = jnp.tile(q_norm[l], ATTN_WIDTH // HEAD_DIM).reshape(1, ATTN_WIDTH)
        kn = jnp.tile(k_norm[l], ATTN_WIDTH // HEAD_DIM).reshape(1, ATTN_WIDTH)
        q, k, v = _attn_prep(U, cos, sin, qn, kn, ones, tm)
        O = _flash(q, k, v, Lc, tq, tk)

        gn = jnp.stack([gn_g[l].reshape(H, HEAD_DIM), gn_b[l].reshape(H, HEAD_DIM)])
        rt = _split_param(jnp.pad(w_router[l], ((0, 0), (0, LANES - N_EXPERTS))))
        rtt = _split_param(w_router[l].T)
        x1, h2, aff, afft = _merge(xs, mods, Y, P, O, U, gn, norm_ffn[l], w_branch_rwkv[l].astype(BF16),
                                   w_branch_attn[l].astype(BF16), w_out[l].astype(BF16), rt, rtt, Lc, tm)

        last = l == depth - 1
        h_flat = h2.reshape(B * Tc, D)
        ye, idx, tab, cap = _moe_set(afft, h_flat, l, w_gate, w_up, w_down, Tc, Lc, T, tm)
        if last:
            xs = _combine(ye, idx, tab, x1, aff, mods, norm_final, Lc, T, cap, tm, False, True)
        else:
            xs = _combine(ye, idx, tab, x1, aff, mods, norm_final, 0, Tc, cap, tm, False, False)
            ye, idx, tab, cap = _moe_set(afft, h_flat, l, w_gate, w_up, w_down, Tc, 0, Lc, tm)
            xs = _combine(ye, idx, tab, x1, aff, mods, norm_final, 0, Lc, cap, tm, True, False, into=xs)
    return xs
```

```python
import functools
import math

import jax
import jax.numpy as jnp
from jax import lax
from jax.experimental import pallas as pl
from jax.experimental.pallas import tpu as pltpu

F32 = jnp.float32
BF16 = jnp.bfloat16

HEAD_DIM = 64
RWKV_HEADS = 8
RWKV_WIDTH = RWKV_HEADS * HEAD_DIM
LORA_DECAY = 64
LORA_ICLR = 64
LORA_GATE = 160
DECAY_SCALE = math.exp(-0.5)
GN_EPS = 64e-5
Q_HEADS = 8
KV_HEADS = 2
Q_PER_KV = Q_HEADS // KV_HEADS
ATTN_WIDTH = Q_HEADS * HEAD_DIM
KV_WIDTH = KV_HEADS * HEAD_DIM
ATTN_SCALE = HEAD_DIM ** -0.5
GRID_W = 64
ROPE_BASE = 10000.0
ROPE_PAIRS = HEAD_DIM // 4
N_EXPERTS = 16
EC_CAPACITY = 2
NORM_EPS = 1e-6

VMEM_LIMIT_BYTES = 56 * 1024 * 1024
LANES = 128
SUBLANES = 8

GATE_OFF = 0
GATE_COLS = 2048
RW_OFF = 2048
RW_COLS = 2048
RW_GLO = 1536
RW_DLO = 1792
RW_ALO = 1920
Q_OFF = 4096
KV_OFF = 4608
PACKED_COLS = 4864

SCAN_CHUNK = 64
ROW_TILE = 256
N_SLOTS = 11
FLASH_UNIT_ROWS = 256
SCAN_GROUP = 2
SCAN_SEQ = 4


def _cparams(*sem):
    return pltpu.CompilerParams(dimension_semantics=sem, vmem_limit_bytes=VMEM_LIMIT_BYTES)


def _split2(a):
    hi = a.astype(BF16)
    lo = (a - hi.astype(F32)).astype(BF16)
    return hi, lo


def _dot(a, b):
    return jnp.dot(a, b, preferred_element_type=F32)


def _dot_nt(a, b):
    return lax.dot_general(a, b, (((1,), (1,)), ((), ())), preferred_element_type=F32)


def _dot_tn(a, b):
    return lax.dot_general(a, b, (((0,), (0,)), ((), ())), preferred_element_type=F32)


def _dot3(a, b_hi, b_lo, dot=_dot):
    a_hi, a_lo = _split2(a)
    return dot(a_hi, b_hi) + dot(a_hi, b_lo) + dot(a_lo, b_hi)


def _dot3f(a, b, dot=_dot):
    b_hi, b_lo = _split2(b)
    return _dot3(a, b_hi, b_lo, dot)


def _segsum(a, ones_bf16):
    a_hi, a_lo = _split2(a)
    return _dot(a_hi, ones_bf16) + _dot(a_lo, ones_bf16)


def _sigmoid(x):
    return 1.0 / (1.0 + jnp.exp(-x))


def _select_mod(mod_ref, k, is_ctx):
    return jnp.where(is_ctx, mod_ref[0, k:k + 1, :], mod_ref[1, k:k + 1, :])


def _mods_kernel(c_ref, w_ref, b_ref, o_ref):
    cv = c_ref[...]
    s = cv * _sigmoid(cv)
    o_ref[...] = _dot3f(s, w_ref[...]) + b_ref[...]


def _mods(cvecs, w_mod, b_mod):
    D, N = w_mod.shape
    tn = N // 4
    return pl.pallas_call(
        _mods_kernel,
        grid=(N // tn,),
        in_specs=[pl.BlockSpec((SUBLANES, D), lambda j: (0, 0)),
                  pl.BlockSpec((D, tn), lambda j: (0, j)),
                  pl.BlockSpec((1, tn), lambda j: (0, j))],
        out_specs=pl.BlockSpec((SUBLANES, tn), lambda j: (0, j)),
        out_shape=jax.ShapeDtypeStruct((SUBLANES, N), F32),
        compiler_params=_cparams("arbitrary"),
    )(cvecs, w_mod, b_mod.reshape(1, N))


def _inproj_kernel(ctx_len, x_ref, mod_ref, nw_ref, w_ref, o_ref):
    tm = x_ref.shape[0]
    row = pl.program_id(2) * tm + lax.broadcasted_iota(jnp.int32, (tm, 1), 0)
    is_ctx = row < ctx_len
    x = x_ref[...]
    h = x * lax.rsqrt(jnp.mean(x * x, axis=-1, keepdims=True) + NORM_EPS) * nw_ref[...]
    h = h * (1.0 + _select_mod(mod_ref, 1, is_ctx)) + _select_mod(mod_ref, 0, is_ctx)
    o_ref[...] = _dot(h.astype(BF16), w_ref[...])


def _inproj(xs, mods, norm_w, w_packed, ctx_len, tm):
    B, Tc, D = xs.shape
    N = w_packed.shape[1]
    tn = N // 2
    return pl.pallas_call(
        functools.partial(_inproj_kernel, ctx_len),
        grid=(N // tn, B, Tc // tm),
        in_specs=[pl.BlockSpec((None, tm, D), lambda n, b, i: (b, i, 0)),
                  pl.BlockSpec((None, 2, SUBLANES, D), lambda n, b, i: (b, 0, 0, 0)),
                  pl.BlockSpec((1, D), lambda n, b, i: (0, 0)),
                  pl.BlockSpec((D, tn), lambda n, b, i: (0, n))],
        out_specs=pl.BlockSpec((None, tm, tn), lambda n, b, i: (b, i, n)),
        out_shape=jax.ShapeDtypeStruct((B, Tc, N), F32),
        compiler_params=_cparams("arbitrary", "arbitrary", "arbitrary"),
    )(xs, mods, norm_w.reshape(1, D), w_packed)


def _rwkv_prep_kernel(ctx_len, total_len, u_ref, up_ref, un_ref, cw_ref, vec_ref,
                      g2h_ref, g2l_ref, dwh_ref, dwl_ref, iah_ref, ial_ref, ones_ref,
                      p_ref, pc_ref):
    tm = u_ref.shape[0]
    C = SCAN_CHUNK
    W = RWKV_WIDTH
    t0 = pl.program_id(1) * tm
    prev_ok = jnp.logical_and(t0 != 0, t0 != ctx_len).astype(F32)
    next_ok = jnp.logical_and(t0 + tm != ctx_len, t0 + tm != total_len).astype(F32)
    row = lax.broadcasted_iota(jnp.int32, (tm, 1), 0)

    u = u_ref[...]
    prev_row = up_ref[SUBLANES - 1:SUBLANES, :] * prev_ok
    next_row = un_ref[0:1, :] * next_ok
    up = jnp.where(row == 0, prev_row, pltpu.roll(u, 1, 0))
    dn = jnp.where(row == tm - 1, next_row, pltpu.roll(u, tm - 1, 0))
    uc = up * cw_ref[0:1, :] + u * cw_ref[1:2, :] + dn * cw_ref[2:3, :]

    r = uc[:, 0:W]
    k = uc[:, W:2 * W]
    v = uc[:, 2 * W:3 * W]
    ones = ones_ref[...]
    key_k = vec_ref[4:5, :]
    key_a = vec_ref[5:6, :]
    rho = vec_ref[6:7, :]

    g = _dot3(_sigmoid(uc[:, RW_GLO:RW_GLO + 256]), g2h_ref[...], g2l_ref[...])
    dl = jnp.tanh(uc[:, RW_DLO:RW_DLO + 128])
    al = uc[:, RW_ALO:RW_ALO + 128]

    kk = k * key_k
    kk = kk * lax.rsqrt(_segsum(kk * kk, ones) + 1e-12)

    ti = lax.broadcasted_iota(jnp.int32, (tm, tm), 0)
    si = lax.broadcasted_iota(jnp.int32, (tm, tm), 1)
    same_chunk = (ti // C) == (si // C)
    tri = (jnp.logical_and(same_chunk, si <= ti).astype(BF16),
           jnp.logical_and(same_chunk, si >= ti).astype(BF16))

    streams = [v, g, None]
    krep_sum = None
    for z in range(2):
        d = vec_ref[z:z + 1, :] + _dot3(dl, dwh_ref[z], dwl_ref[z])
        logw = -DECAY_SCALE * _sigmoid(d)
        a = _sigmoid(vec_ref[2 + z:3 + z, :] + _dot3(al, iah_ref[z], ial_ref[z]))
        krep = k * (1.0 + (a - 1.0) * key_a)
        ka = kk * a
        krep_sum = krep if krep_sum is None else krep_sum + krep
        l_hi = logw.astype(BF16)
        rem = logw - l_hi.astype(F32)
        l_mid = rem.astype(BF16)
        l_lo = (rem - l_mid.astype(F32)).astype(BF16)
        cum = _dot(tri[z], l_hi) + _dot(tri[z], l_mid) + _dot(tri[z], l_lo)
        e_neg = jnp.exp(-cum)
        streams += [kk * jnp.exp(cum - logw), ka * e_neg, krep * e_neg, r * jnp.exp(cum)]
        for j in range(tm // C):
            last = j * C + (C - 1 if z == 0 else 0)
            pc = jnp.exp(cum[last:last + 1, :])
            for h in range(RWKV_HEADS):
                pc_ref[z, j, h:h + 1, :] = pc[:, h * HEAD_DIM:(h + 1) * HEAD_DIM]
    bonus = _segsum(r * krep_sum * rho, ones)
    streams[2] = bonus * v
    for s, val in enumerate(streams):
        for h in range(RWKV_HEADS):
            p_ref[s, h] = val[:, h * HEAD_DIM:(h + 1) * HEAD_DIM]


def _rwkv_prep(U, conv_packed, vecs, g2, dw2, ia2, ones, ctx_len, tm):
    B, Tc, _ = U.shape
    C = SCAN_CHUNK
    H = RWKV_HEADS
    rb = RW_OFF // RW_COLS
    nt = Tc // tm
    hb = tm // SUBLANES
    full = lambda a: pl.BlockSpec(a.shape, lambda b, i: (0,) * a.ndim)
    args = (conv_packed, vecs, g2[0], g2[1], dw2[0], dw2[1], ia2[0], ia2[1], ones)
    return pl.pallas_call(
        functools.partial(_rwkv_prep_kernel, ctx_len, Tc),
        grid=(B, nt),
        in_specs=[pl.BlockSpec((None, tm, RW_COLS), lambda b, i: (b, i, rb)),
                  pl.BlockSpec((None, SUBLANES, RW_COLS),
                               lambda b, i: (b, jnp.maximum(i * hb - 1, 0), rb)),
                  pl.BlockSpec((None, SUBLANES, RW_COLS),
                               lambda b, i: (b, jnp.minimum((i + 1) * hb, Tc // SUBLANES - 1), rb))]
                 + [full(a) for a in args],
        out_specs=[pl.BlockSpec((None, N_SLOTS, H, tm, HEAD_DIM), lambda b, i: (b, 0, 0, i, 0)),
                   pl.BlockSpec((None, 2, tm // C, H, HEAD_DIM), lambda b, i: (b, 0, i, 0, 0))],
        out_shape=[jax.ShapeDtypeStruct((B, N_SLOTS, H, Tc, HEAD_DIM), F32),
                   jax.ShapeDtypeStruct((B, 2, Tc // C, H, HEAD_DIM), F32)],
        compiler_params=_cparams("arbitrary", "arbitrary"),
    )(U, U, U, *args)


def _scan_local_kernel(v_ref, a_ref, b_ref, k_ref, r_ref, pc_ref, rb_ref, yb_ref, m_ref, hm_ref):
    C = SCAN_CHUNK
    rev = pl.program_id(1) == 1
    t2 = lax.broadcasted_iota(jnp.int32, (C, 2 * C), 0)
    s2 = lax.broadcasted_iota(jnp.int32, (C, 2 * C), 1)
    right = s2 >= C
    s2 = jnp.where(right, s2 - C, s2)
    d2 = jnp.where(rev, t2 - s2, s2 - t2)
    strict_right = jnp.logical_and(d2 < 0, right)
    incl2 = d2 <= 0
    t1 = lax.broadcasted_iota(jnp.int32, (C, C), 0)
    s1 = lax.broadcasted_iota(jnp.int32, (C, C), 1)
    strict1 = jnp.where(rev, t1 - s1, s1 - t1) < 0
    ident = (t1 == s1).astype(F32)

    units = [(c, h) for c in range(SCAN_GROUP) for h in range(RWKV_HEADS)]
    n = range(len(units))
    rows = lambda ref, c, h: ref[h, c * C:(c + 1) * C, :]
    V = [rows(v_ref, c, h) for c, h in units]
    A = [rows(a_ref, c, h) for c, h in units]
    Bm = [rows(b_ref, c, h) for c, h in units]
    R = [rows(r_ref, c, h) for c, h in units]
    AR = [jnp.concatenate([A[i], R[i]], axis=0).astype(BF16) for i in n]
    BK = [jnp.concatenate([Bm[i], rows(k_ref, c, h)], axis=0).astype(BF16) for i, (c, h) in enumerate(units)]
    pc = [pc_ref[c, h:h + 1, :] for c, h in units]
    mm = lambda a, b, dot=_dot: dot(a.astype(BF16), b.astype(BF16))
    G = [_dot_nt(AR[i], BK[i]) for i in n]
    n_ab = [jnp.where(strict1, G[i][:C, :C], 0.0) for i in n]
    n_ak_wide = [jnp.where(strict_right, G[i][:C], 0.0) for i in n]
    n_r_wide = [jnp.where(incl2, jnp.where(right, G[i][C:], -G[i][C:]), 0.0) for i in n]
    nv = [mm(n_ak_wide[i], jnp.concatenate([V[i], V[i]], axis=0)) for i in n]

    inv = [ident - n_ab[i] for i in n]
    power = n_ab
    for _ in range(5):
        pb = [power[i].astype(BF16) for i in n]
        power = [_dot(pb[i], pb[i]) for i in n]
        inv = [_dot(inv[i].astype(BF16), (ident + power[i]).astype(BF16)) for i in n]

    ab = [mm(inv[i], A[i]) for i in n]
    ub = [mm(inv[i], nv[i]) for i in n]
    for i, (c, h) in enumerate(units):
        rb_ref[h, c * C:(c + 1) * C, :] = R[i] + mm(n_r_wide[i][:, :C], ab[i])
    for i, (c, h) in enumerate(units):
        yb_ref[h, c * C:(c + 1) * C, :] = mm(n_r_wide[i], jnp.concatenate([ub[i], V[i]], axis=0))
    for i, (c, h) in enumerate(units):
        m_ref[c, h] = (ident - mm(ab[i], Bm[i], _dot_tn)) * pc[i]
    for i, (c, h) in enumerate(units):
        W = jnp.concatenate([-ub[i], V[i]], axis=0).astype(BF16)
        hm_ref[c, h] = _dot_tn(W, BK[i]) * pc[i]


def _scan_seq_kernel(rb_ref, yb_ref, m_ref, hm_ref, y_ref, state_ref):
    C = SCAN_CHUNK
    rev = pl.program_id(1) == 1

    @pl.when(pl.program_id(2) == 0)
    def _():
        state_ref[...] = jnp.zeros_like(state_ref)

    heads = range(RWKV_HEADS)
    S = [state_ref[h] for h in heads]
    for step in range(SCAN_SEQ):
        c = jnp.where(rev, SCAN_SEQ - 1 - step, step)
        r0 = pl.multiple_of(c * C, C)
        Ss = [_split2(S[h]) for h in heads]
        for h in heads:
            y_ref[h, pl.ds(r0, C), :] = (_dot3(rb_ref[h, pl.ds(r0, C), :], Ss[h][0], Ss[h][1], _dot_nt)
                                         + yb_ref[h, pl.ds(r0, C), :])
        m = [_split2(m_ref[c, h]) for h in heads]
        S = [_dot(Ss[h][0], m[h][0]) + _dot(Ss[h][0], m[h][1]) + _dot(Ss[h][1], m[h][0]) + hm_ref[c, h]
             for h in heads]
    for h in heads:
        state_ref[h] = S[h]


def _scan(P, PC, ctx_len):
    B, _, H, Tc, _ = P.shape
    C = SCAN_CHUNK
    nch = Tc // C
    assert nch % SCAN_GROUP == 0 and nch % SCAN_SEQ == 0 and (ctx_len // C) % SCAN_SEQ == 0
    gc = SCAN_GROUP * C
    tok = lambda slot: pl.BlockSpec((None, None, H, gc, HEAD_DIM), lambda b, z, j: (b, slot(z), 0, j, 0))
    mat_shape = jax.ShapeDtypeStruct((2, B, nch, H, HEAD_DIM, HEAD_DIM), F32)
    tok_shape = jax.ShapeDtypeStruct((2, B, H, Tc, HEAD_DIM), F32)
    RB, YB, MM, HM = pl.pallas_call(
        _scan_local_kernel,
        grid=(B, 2, nch // SCAN_GROUP),
        in_specs=[tok(lambda z: 0), tok(lambda z: 3 + 4 * z), tok(lambda z: 4 + 4 * z), tok(lambda z: 5 + 4 * z),
                  tok(lambda z: 6 + 4 * z),
                  pl.BlockSpec((None, None, SCAN_GROUP, H, HEAD_DIM), lambda b, z, j: (b, z, j, 0, 0))],
        out_specs=[pl.BlockSpec((None, None, H, gc, HEAD_DIM), lambda b, z, j: (z, b, 0, j, 0)),
                   pl.BlockSpec((None, None, H, gc, HEAD_DIM), lambda b, z, j: (z, b, 0, j, 0)),
                   pl.BlockSpec((None, None, SCAN_GROUP, H, HEAD_DIM, HEAD_DIM), lambda b, z, j: (z, b, j, 0, 0, 0)),
                   pl.BlockSpec((None, None, SCAN_GROUP, H, HEAD_DIM, HEAD_DIM), lambda b, z, j: (z, b, j, 0, 0, 0))],
        out_shape=[tok_shape, tok_shape, mat_shape, mat_shape],
        compiler_params=_cparams("arbitrary", "arbitrary", "arbitrary"),
    )(P, P, P, P, P, PC)

    nblk = nch // SCAN_SEQ
    nctx = ctx_len // C // SCAN_SEQ

    def block(z, j):
        rev_j = jnp.where(j < nctx, nctx - 1 - j, nblk + nctx - 1 - j)
        return jnp.where(z == 1, rev_j, j)

    sc = SCAN_SEQ * C
    tok_spec = pl.BlockSpec((None, None, H, sc, HEAD_DIM), lambda b, z, j: (z, b, 0, block(z, j), 0))
    mat_spec = pl.BlockSpec((None, None, SCAN_SEQ, H, HEAD_DIM, HEAD_DIM),
                            lambda b, z, j: (z, b, block(z, j), 0, 0, 0))
    return pl.pallas_call(
        _scan_seq_kernel,
        grid=(B, 2, nblk),
        in_specs=[tok_spec, tok_spec, mat_spec, mat_spec],
        out_specs=tok_spec,
        out_shape=tok_shape,
        scratch_shapes=[pltpu.VMEM((H, HEAD_DIM, HEAD_DIM), F32)],
        compiler_params=_cparams("arbitrary", "arbitrary", "arbitrary"),
    )(RB, YB, MM, HM)


def _attn_prep_kernel(uq_ref, ukv_ref, cos_ref, sin_ref, qn_ref, kn_ref, ones_ref, q_ref, k_ref, v_ref):
    ones = ones_ref[...]
    cos = cos_ref[...]
    sin = sin_ref[...]
    lane = lax.broadcasted_iota(jnp.int32, (1, ATTN_WIDTH), 1)
    first_half = (lane % (2 * ROPE_PAIRS)) < ROPE_PAIRS

    def norm_rope(x, gain, width):
        ms = _segsum(x * x, ones[:width, :width]) * (1.0 / HEAD_DIM)
        y = x * lax.rsqrt(ms + NORM_EPS) * gain
        partner = jnp.where(first_half[:, :width],
                            pltpu.roll(y, width - ROPE_PAIRS, 1), pltpu.roll(y, ROPE_PAIRS, 1))
        return y * cos[:, :width] + partner * sin[:, :width]

    q = norm_rope(uq_ref[...], qn_ref[...], ATTN_WIDTH) * (ATTN_SCALE * math.log2(math.e))
    ukv = ukv_ref[...]
    k = norm_rope(ukv[:, :KV_WIDTH], kn_ref[:, :KV_WIDTH], KV_WIDTH)
    v = ukv[:, KV_WIDTH:]
    low_lanes = lax.broadcasted_iota(jnp.int32, (1, KV_WIDTH), 1) < HEAD_DIM
    for g in range(KV_HEADS):
        for j in range(Q_PER_KV):
            o = (g * Q_PER_KV + j) * HEAD_DIM
            q_ref[g, j] = q[:, o:o + HEAD_DIM].astype(BF16)
        k_ref[g] = k[:, g * HEAD_DIM:(g + 1) * HEAD_DIM].astype(BF16)
        v_g = v if g == 0 else pltpu.roll(v, (KV_HEADS - g) * HEAD_DIM, 1)
        v_ref[g] = jnp.where(low_lanes, v_g, 1.0).astype(BF16)


def _attn_prep(U, cos, sin, qn, kn, ones, tm):
    B, Tc, _ = U.shape
    full = lambda a: pl.BlockSpec(a.shape, lambda b, i: (0,) * a.ndim)
    return pl.pallas_call(
        _attn_prep_kernel,
        grid=(B, Tc // tm),
        in_specs=[pl.BlockSpec((None, tm, ATTN_WIDTH), lambda b, i: (b, i, Q_OFF // ATTN_WIDTH)),
                  pl.BlockSpec((None, tm, 2 * KV_WIDTH), lambda b, i: (b, i, KV_OFF // (2 * KV_WIDTH))),
                  pl.BlockSpec((tm, ATTN_WIDTH), lambda b, i: (i, 0)),
                  pl.BlockSpec((tm, ATTN_WIDTH), lambda b, i: (i, 0)),
                  full(qn), full(kn), full(ones)],
        out_specs=[pl.BlockSpec((None, KV_HEADS, Q_PER_KV, tm, HEAD_DIM), lambda b, i: (b, 0, 0, i, 0)),
                   pl.BlockSpec((None, KV_HEADS, tm, HEAD_DIM), lambda b, i: (b, 0, i, 0)),
                   pl.BlockSpec((None, KV_HEADS, tm, 2 * HEAD_DIM), lambda b, i: (b, 0, i, 0))],
        out_shape=[jax.ShapeDtypeStruct((B, KV_HEADS, Q_PER_KV, Tc, HEAD_DIM), BF16),
                   jax.ShapeDtypeStruct((B, KV_HEADS, Tc, HEAD_DIM), BF16),
                   jax.ShapeDtypeStruct((B, KV_HEADS, Tc, 2 * HEAD_DIM), BF16)],
        compiler_params=_cparams("arbitrary", "arbitrary"),
    )(U, U, cos, sin, qn, kn, ones)


def _flash_kernel(ctx_len, q_ref, k_ref, v_ref, o_ref, m_ref, acc_ref):
    G, tq, _ = q_ref.shape
    tk = k_ref.shape[0]
    qi = pl.program_id(2)
    ki = pl.program_id(3)
    has_ctx_rows = qi * tq < ctx_len

    @pl.when(ki == 0)
    def _():
        m_ref[...] = jnp.full_like(m_ref, -jnp.inf)
        acc_ref[...] = jnp.zeros_like(acc_ref)

    def step(masked):
        k = k_ref[...]
        v = v_ref[...]
        rows = FLASH_UNIT_ROWS
        units = [(j, r0) for j in range(G) for r0 in range(0, tq, rows)]

        def logits(u):
            j, r0 = u
            s = _dot_nt(q_ref[j, r0:r0 + rows, :], k)
            if masked:
                key = ki * tk + lax.broadcasted_iota(jnp.int32, (1, tk), 1)
                row = qi * tq + r0 + lax.broadcasted_iota(jnp.int32, (rows, 1), 0)
                s = jnp.where(jnp.logical_or(row >= ctx_len, key < ctx_len), s, -jnp.inf)
            return s

        s_next = logits(units[0])
        for n, (j, r0) in enumerate(units):
            s = s_next
            if n + 1 < len(units):
                s_next = logits(units[n + 1])
            m_old = m_ref[j, r0:r0 + rows, :]
            m_new = jnp.maximum(m_old, jnp.max(s, axis=-1, keepdims=True))
            p = jnp.exp2(s - m_new).astype(BF16)
            acc_ref[j, r0:r0 + rows, :] = jnp.exp2(m_old - m_new) * acc_ref[j, r0:r0 + rows, :] + _dot(p, v)
            m_ref[j, r0:r0 + rows, :] = m_new

    @pl.when(jnp.logical_not(has_ctx_rows))
    def _():
        step(False)

    @pl.when(has_ctx_rows)
    def _():
        step(True)

    @pl.when(ki == pl.num_programs(3) - 1)
    def _():
        for j in range(G):
            acc = acc_ref[j]
            o_ref[j] = acc[:, :HEAD_DIM] / acc[:, HEAD_DIM:]


def _flash(q, k, v, ctx_len, tq, tk):
    B, KVH, G, Tc, Dh = q.shape
    return pl.pallas_call(
        functools.partial(_flash_kernel, ctx_len),
        grid=(B, KVH, Tc // tq, Tc // tk),
        in_specs=[pl.BlockSpec((None, None, G, tq, Dh), lambda b, g, i, j: (b, g, 0, i, 0)),
                  pl.BlockSpec((None, None, tk, Dh), lambda b, g, i, j: (b, g, j, 0)),
                  pl.BlockSpec((None, None, tk, 2 * Dh), lambda b, g, i, j: (b, g, j, 0))],
        out_specs=pl.BlockSpec((None, None, G, tq, Dh), lambda b, g, i, j: (b, g, 0, i, 0)),
        out_shape=jax.ShapeDtypeStruct((B, KVH, G, Tc, Dh), F32),
        scratch_shapes=[pltpu.VMEM((G, tq, 1), F32), pltpu.VMEM((G, tq, 2 * Dh), F32)],
        compiler_params=_cparams("arbitrary", "arbitrary", "arbitrary", "arbitrary"),
    )(q, k, v)


def _merge_kernel(ctx_len, x_ref, mod_ref, y_ref, g_ref, bv_ref, o_ref, gate_ref, gn_ref, nw_ref,
                  wr_ref, wa_ref, wo_ref, rth_ref, rtl_ref, rtth_ref, rttl_ref,
                  x1_ref, h2_ref, aff_ref, afft_ref):
    tm = x_ref.shape[0]
    D = x_ref.shape[1]
    row = pl.program_id(1) * tm + lax.broadcasted_iota(jnp.int32, (tm, 1), 0)
    is_ctx = row < ctx_len

    rw = []
    for h in range(RWKV_HEADS):
        y = y_ref[0, h] + y_ref[1, h]
        mu = jnp.mean(y, axis=-1, keepdims=True)
        yc = y - mu
        var = jnp.mean(yc * yc, axis=-1, keepdims=True)
        yn = yc * lax.rsqrt(var + GN_EPS) * gn_ref[0, h:h + 1, :] + gn_ref[1, h:h + 1, :]
        rw.append((yn + bv_ref[h]) * g_ref[h])
    y_rwkv = jnp.concatenate(rw, axis=-1).astype(BF16)
    y_attn = jnp.concatenate([o_ref[g, j] for g in range(KV_HEADS) for j in range(Q_PER_KV)],
                             axis=-1).astype(BF16)
    gates = gate_ref[...]
    merged = (_sigmoid(gates[:, :D]) * _dot(y_rwkv, wr_ref[...])
              + _sigmoid(gates[:, D:]) * _dot(y_attn, wa_ref[...]))
    x1 = x_ref[...] + _select_mod(mod_ref, 2, is_ctx) * _dot(merged.astype(BF16), wo_ref[...])
    x1_ref[...] = x1

    h2 = x1 * lax.rsqrt(jnp.mean(x1 * x1, axis=-1, keepdims=True) + NORM_EPS) * nw_ref[...]
    h2 = h2 * (1.0 + _select_mod(mod_ref, 4, is_ctx)) + _select_mod(mod_ref, 3, is_ctx)
    h2_ref[...] = h2
    h_hi, h_lo = _split2(h2)
    rt_hi, rt_lo = rth_ref[...], rtl_ref[...]
    logits = _dot(h_hi, rt_hi) + _dot(h_hi, rt_lo) + _dot(h_lo, rt_hi)
    lane = lax.broadcasted_iota(jnp.int32, (1, LANES), 1)
    logits = jnp.where(lane < N_EXPERTS, logits, -jnp.inf)
    e = jnp.exp(logits - jnp.max(logits, axis=-1, keepdims=True))
    aff_ref[...] = e / jnp.sum(e, axis=-1, keepdims=True)
    rtt_hi, rtt_lo = rtth_ref[...], rttl_ref[...]
    logits_t = _dot_nt(rtt_hi, h_hi) + _dot_nt(rtt_hi, h_lo) + _dot_nt(rtt_lo, h_hi)
    et = jnp.exp(logits_t - jnp.max(logits_t, axis=0, keepdims=True))
    afft_ref[...] = et / jnp.sum(et, axis=0, keepdims=True)


def _merge(xs, mods, Y, P, O, U, gn, norm_w, wr, wa, wo, rt, rtt, ctx_len, tm):
    B, Tc, D = xs.shape
    H = RWKV_HEADS
    full = lambda a: pl.BlockSpec(a.shape, lambda b, i: (0,) * a.ndim)
    nw = norm_w.reshape(1, D)
    return pl.pallas_call(
        functools.partial(_merge_kernel, ctx_len),
        grid=(B, Tc // tm),
        in_specs=[pl.BlockSpec((None, tm, D), lambda b, i: (b, i, 0)),
                  pl.BlockSpec((None, 2, SUBLANES, D), lambda b, i: (b, 0, 0, 0)),
                  pl.BlockSpec((2, None, H, tm, HEAD_DIM), lambda b, i: (0, b, 0, i, 0)),
                  pl.BlockSpec((None, None, H, tm, HEAD_DIM), lambda b, i: (b, 1, 0, i, 0)),
                  pl.BlockSpec((None, None, H, tm, HEAD_DIM), lambda b, i: (b, 2, 0, i, 0)),
                  pl.BlockSpec((None, KV_HEADS, Q_PER_KV, tm, HEAD_DIM), lambda b, i: (b, 0, 0, i, 0)),
                  pl.BlockSpec((None, tm, GATE_COLS), lambda b, i: (b, i, GATE_OFF // GATE_COLS)),
                  full(gn), full(nw), full(wr), full(wa), full(wo), full(rt[0]), full(rt[1]),
                  full(rtt[0]), full(rtt[1])],
        out_specs=[pl.BlockSpec((None, tm, D), lambda b, i: (b, i, 0)),
                   pl.BlockSpec((None, tm, D), lambda b, i: (b, i, 0)),
                   pl.BlockSpec((None, tm, LANES), lambda b, i: (b, i, 0)),
                   pl.BlockSpec((None, N_EXPERTS, tm), lambda b, i: (b, 0, i))],
        out_shape=[jax.ShapeDtypeStruct((B, Tc, D), F32),
                   jax.ShapeDtypeStruct((B, Tc, D), F32),
                   jax.ShapeDtypeStruct((B, Tc, LANES), F32),
                   jax.ShapeDtypeStruct((B, N_EXPERTS, Tc), F32)],
        compiler_params=_cparams("arbitrary", "arbitrary"),
    )(xs, mods, Y, P, P, O, U, gn, nw, wr, wa, wo, rt[0], rt[1], rtt[0], rtt[1])


def _route_kernel(start, n, cap, tile, afft_ref, idx_ref, tab_ref, bits_ref, cnt_ref, sel_ref):
    E, Tc = afft_ref.shape
    nblk = (start + n) // LANES
    tok = lax.broadcasted_iota(jnp.int32, (1, Tc), 1)
    in_set = jnp.logical_and(tok >= start, tok < start + n)
    bits_ref[...] = jnp.where(in_set, lax.bitcast_convert_type(afft_ref[...], jnp.int32), -1)

    def search(i, thr):
        cand = jnp.bitwise_or(thr, jnp.left_shift(1, 30 - i))
        count = jnp.sum(jnp.where(bits_ref[...] >= cand, 1.0, 0.0), axis=1, keepdims=True)
        return jnp.where(count >= cap, cand, thr)

    thr = lax.fori_loop(0, 31, search, jnp.zeros((E, 1), jnp.int32))
    above = jnp.sum(jnp.where(bits_ref[...] > thr, 1.0, 0.0), axis=1, keepdims=True)
    need = cap - above

    r = lax.broadcasted_iota(jnp.int32, (LANES, LANES), 0)
    c = lax.broadcasted_iota(jnp.int32, (LANES, LANES), 1)
    tri = (r <= c).astype(BF16)
    ties_before = jnp.zeros((E, 1), F32)
    taken_before = jnp.zeros((E, 1), F32)
    for j in range(nblk):
        blk = bits_ref[:, j * LANES:(j + 1) * LANES]
        tie = jnp.where(blk == thr, 1.0, 0.0)
        tie_rank = ties_before + _dot(tie.astype(BF16), tri) - tie
        sel = jnp.where(jnp.logical_or(blk > thr, jnp.logical_and(blk == thr, tie_rank < need)), 1.0, 0.0)
        cnt_ref[:, j * LANES:(j + 1) * LANES] = taken_before + _dot(sel.astype(BF16), tri)
        sel_ref[:, j * LANES:(j + 1) * LANES] = sel
        ties_before = ties_before + jnp.sum(tie, axis=1, keepdims=True)
        taken_before = taken_before + jnp.sum(sel, axis=1, keepdims=True)
    if nblk * LANES < Tc:
        cnt_ref[:, nblk * LANES:] = jnp.full((E, Tc - nblk * LANES), float(cap), F32)
        sel_ref[:, nblk * LANES:] = jnp.zeros((E, Tc - nblk * LANES), F32)

    t = lax.broadcasted_iota(jnp.int32, (Tc, LANES), 0)
    k = lax.broadcasted_iota(jnp.int32, (Tc, LANES), 1)
    tab_ref[...] = _dot(sel_ref[...].astype(BF16), (t < k * tile).astype(BF16)).astype(jnp.int32)

    sc = min(cap, 256)
    slot = lax.broadcasted_iota(jnp.int32, (sc, 1), 0).astype(F32)

    for e in range(E):
        for ci in range(cap // sc):
            p_col = slot + float(ci * sc)

            def per_block(j, acc, e=e, p_col=p_col):
                off = pl.multiple_of(j * LANES, LANES)
                return acc + jnp.where(cnt_ref[e:e + 1, pl.ds(off, LANES)] <= p_col, 1.0, 0.0)

            acc = lax.fori_loop(0, nblk, per_block, jnp.zeros((sc, LANES), F32))
            idx_ref[e, ci * sc:(ci + 1) * sc, :] = jnp.sum(acc, axis=1, keepdims=True).astype(jnp.int32)


def _route(afft, start, n, cap, tile):
    B, E, Tc = afft.shape
    return pl.pallas_call(
        functools.partial(_route_kernel, start, n, cap, tile),
        grid=(B,),
        in_specs=[pl.BlockSpec((None, E, Tc), lambda b: (b, 0, 0))],
        out_specs=[pl.BlockSpec((None, E, cap, 1), lambda b: (b, 0, 0, 0)),
                   pl.BlockSpec((None, E, LANES), lambda b: (b, 0, 0))],
        out_shape=[jax.ShapeDtypeStruct((B, E, cap, 1), jnp.int32),
                   jax.ShapeDtypeStruct((B, E, LANES), jnp.int32)],
        scratch_shapes=[pltpu.VMEM((E, Tc), jnp.int32), pltpu.VMEM((E, Tc), F32), pltpu.VMEM((E, Tc), F32)],
        compiler_params=_cparams("arbitrary"),
    )(afft)


def _ffn_kernel(n_steps, rows_ref, h_hbm, wg_ref, wu_ref, wd_ref, o_ref, xf_ref, xb_ref, acc_ref, sem):
    f = pl.program_id(2)
    tm = xb_ref.shape[0]
    tile = pl.program_id(0) * pl.num_programs(1) + pl.program_id(1)
    last_tile = pl.num_programs(0) * pl.num_programs(1) - 1
    slot = tile % 2

    def row_copy(buf, r, src_row):
        return pltpu.make_async_copy(h_hbm.at[pl.ds(src_row, 1)], xf_ref.at[buf, pl.ds(r, 1)], sem.at[buf])

    @pl.when(jnp.logical_and(tile == 0, f == 0))
    def _():
        def start(r, carry):
            row_copy(0, r, rows_ref[r]).start()
            return carry
        lax.fori_loop(0, tm, start, 0)

    next_base = jnp.minimum(tile + 1, last_tile) * tm
    per_step = tm // n_steps

    @pl.when(f == 0)
    def _():
        for r in range(tm):
            row_copy(slot, r, 0).wait()
        xb_ref[...] = xf_ref[slot].astype(BF16)
        acc_ref[...] = jnp.zeros_like(acc_ref)
        for r in range(per_step * n_steps, tm):
            row_copy(1 - slot, r, rows_ref[next_base + r]).start()

    for i in range(per_step):
        r = i * n_steps + f
        row_copy(1 - slot, r, rows_ref[next_base + r]).start()

    x = xb_ref[...]
    a = _dot(x, wg_ref[...].astype(BF16))
    b = _dot(x, wu_ref[...].astype(BF16))
    hid = (a * _sigmoid(a) * b).astype(BF16)
    acc_ref[...] += _dot(hid, wd_ref[...].astype(BF16))

    @pl.when(f == n_steps - 1)
    def _():
        o_ref[...] = acc_ref[...]

    @pl.when(jnp.logical_and(tile == last_tile, f == n_steps - 1))
    def _():
        for r in range(tm):
            row_copy(1 - slot, r, 0).wait()


def _expert_ffn(rows, h_flat, layer, w_gate, w_up, w_down, tm, tf):
    _, E, D, F = w_gate.shape
    M = rows.shape[0] // E
    n_steps = F // tf
    grid_spec = pltpu.PrefetchScalarGridSpec(
        num_scalar_prefetch=1,
        grid=(E, M // tm, n_steps),
        in_specs=[pl.BlockSpec(memory_space=pl.ANY),
                  pl.BlockSpec((None, None, D, tf), lambda e, i, f, rows: (layer, e, 0, f)),
                  pl.BlockSpec((None, None, D, tf), lambda e, i, f, rows: (layer, e, 0, f)),
                  pl.BlockSpec((None, None, tf, D), lambda e, i, f, rows: (layer, e, f, 0))],
        out_specs=pl.BlockSpec((None, tm, D), lambda e, i, f, rows: (e, i, 0)),
        scratch_shapes=[pltpu.VMEM((2, tm, D), F32), pltpu.VMEM((tm, D), BF16), pltpu.VMEM((tm, D), F32),
                        pltpu.SemaphoreType.DMA((2,))])
    return pl.pallas_call(
        functools.partial(_ffn_kernel, n_steps),
        grid_spec=grid_spec,
        out_shape=jax.ShapeDtypeStruct((E, M, D), F32),
        compiler_params=_cparams("arbitrary", "arbitrary", "arbitrary"),
    )(rows, h_flat, w_gate, w_up, w_down)


def _combine_kernel(set_start, cap, is_ctx, final, idx_ref, tab_ref, ye_hbm, x1_ref, aff_ref, mod_ref, nw_ref,
                    *rest):
    o_ref, stage_ref, sem = rest[-3:]
    E = N_EXPERTS
    tm = x1_ref.shape[0]
    b = pl.program_id(0)
    k = pl.program_id(1) + set_start // tm
    t0 = k * tm
    stage_ref[...] = jnp.zeros_like(stage_ref)

    def row_copy(e, s):
        tok = idx_ref[(b * E + e) * cap + s] - t0
        return pltpu.make_async_copy(ye_hbm.at[e, pl.ds(b * cap + s, 1)], stage_ref.at[e, pl.ds(tok, 1)], sem)

    bounds = [(tab_ref[(b * E + e) * LANES + k], tab_ref[(b * E + e) * LANES + k + 1]) for e in range(E)]
    for e in range(E):
        def start(s, carry, e=e):
            row_copy(e, s).start()
            return carry
        lax.fori_loop(bounds[e][0], bounds[e][1], start, 0)
    for e in range(E):
        count = bounds[e][1] - bounds[e][0]
        bulk = pl.multiple_of((count // SUBLANES) * SUBLANES, SUBLANES)

        @pl.when(bulk > 0)
        def _(e=e, bulk=bulk):
            pltpu.make_async_copy(ye_hbm.at[e, pl.ds(0, bulk)], stage_ref.at[e, pl.ds(0, bulk)], sem).wait()

        def wait(s, carry, e=e):
            row_copy(e, s).wait()
            return carry
        lax.fori_loop(bounds[e][0] + bulk, bounds[e][1], wait, 0)

    aff = aff_ref[...]
    total = stage_ref[0] * aff[:, 0:1]
    for e in range(1, E):
        total = total + stage_ref[e] * aff[:, e:e + 1]
    x = x1_ref[...] + mod_ref[0 if is_ctx else 1, 5:6, :] * total
    if final:
        x = x * lax.rsqrt(jnp.mean(x * x, axis=-1, keepdims=True) + NORM_EPS) * nw_ref[...]
    o_ref[...] = x


def _combine(ye, idx, tab, x1, aff, mods, norm_w, set_start, n, cap, tm, is_ctx, final, into=None):
    B, Tc, D = x1.shape
    E = N_EXPERTS
    off = set_start // tm
    in_specs = [pl.BlockSpec(memory_space=pl.ANY),
                pl.BlockSpec((None, tm, D), lambda b, i, *_: (b, i + off, 0)),
                pl.BlockSpec((None, tm, LANES), lambda b, i, *_: (b, i + off, 0)),
                pl.BlockSpec((None, 2, SUBLANES, D), lambda b, i, *_: (b, 0, 0, 0)),
                pl.BlockSpec((1, D), lambda b, i, *_: (0, 0))]
    args = [idx.reshape(-1), tab.reshape(-1), ye, x1, aff, mods, norm_w.reshape(1, D)]
    aliases = {}
    if into is not None:
        in_specs.append(pl.BlockSpec(memory_space=pl.ANY))
        aliases = {len(args): 0}
        args.append(into)
    if final:
        out_shape = jax.ShapeDtypeStruct((B, n, D), F32)
        out_spec = pl.BlockSpec((None, tm, D), lambda b, i, *_: (b, i, 0))
    else:
        out_shape = jax.ShapeDtypeStruct((B, Tc, D), F32)
        out_spec = pl.BlockSpec((None, tm, D), lambda b, i, *_: (b, i + off, 0))
    grid_spec = pltpu.PrefetchScalarGridSpec(
        num_scalar_prefetch=2, grid=(B, n // tm), in_specs=in_specs, out_specs=out_spec,
        scratch_shapes=[pltpu.VMEM((E, tm, D), F32), pltpu.SemaphoreType.DMA(())])
    return pl.pallas_call(
        functools.partial(_combine_kernel, set_start, cap, is_ctx, final),
        grid_spec=grid_spec,
        out_shape=out_shape,
        input_output_aliases=aliases,
        compiler_params=_cparams("arbitrary", "arbitrary"),
    )(*args)


def _moe_set(afft, h_flat, layer, w_gate, w_up, w_down, Tc, start, n, tile):
    B = afft.shape[0]
    E = N_EXPERTS
    cap = EC_CAPACITY * n // E
    idx, tab = _route(afft, start, n, cap, tile)
    idx = idx.reshape(B, E, cap)
    rows = idx + (jnp.arange(B, dtype=jnp.int32) * Tc)[:, None, None]
    rows = jnp.swapaxes(rows, 0, 1).reshape(-1)
    ye = _expert_ffn(rows, h_flat, layer, w_gate, w_up, w_down, min(1024, B * cap), 256)
    return ye, idx, tab, cap


def _pack_cols(w):
    W = RWKV_WIDTH
    lead = w.shape[:-1]
    zeros = lambda n: jnp.zeros(lead + (n,), w.dtype)
    rk = 3 * W
    rw_end = rk + LORA_GATE + 2 * LORA_DECAY + 2 * LORA_ICLR
    gqa_end = rw_end + ATTN_WIDTH + 2 * KV_WIDTH
    return jnp.concatenate([
        w[..., gqa_end:],
        w[..., :rk],
        w[..., rk:rk + LORA_GATE], zeros(256 - LORA_GATE),
        w[..., rk + LORA_GATE:rw_end],
        w[..., rw_end:gqa_end],
    ], axis=-1)


def _rope_tables(n_tokens, ctx_len, width):
    rows = n_tokens // GRID_W
    row = jnp.repeat(jnp.arange(rows, dtype=F32), GRID_W)
    col = jnp.broadcast_to(jnp.arange(GRID_W, dtype=F32)[None, :], (rows, GRID_W)).reshape(-1)
    inv = jnp.power(ROPE_BASE, -jnp.arange(ROPE_PAIRS, dtype=F32) / ROPE_PAIRS)
    ang = jnp.stack([row[:, None] * inv, col[:, None] * inv], axis=1)
    cos = jnp.repeat(jnp.cos(ang)[:, :, None, :], 2, axis=2).reshape(n_tokens, HEAD_DIM)
    sin = jnp.sin(ang)
    sin = jnp.stack([-sin, sin], axis=2).reshape(n_tokens, HEAD_DIM)
    cos = jnp.concatenate([jnp.ones((ctx_len, HEAD_DIM), F32), cos], axis=0)
    sin = jnp.concatenate([jnp.zeros((ctx_len, HEAD_DIM), F32), sin], axis=0)
    reps = width // HEAD_DIM
    return jnp.tile(cos, (1, reps)), jnp.tile(sin, (1, reps))


def _pad_rows(w, rows, offset=0):
    out = jnp.zeros((rows,) + w.shape[1:], w.dtype)
    return out.at[offset:offset + w.shape[0]].set(w)


def _split_param(w):
    hi = w.astype(BF16)
    return hi, (w - hi.astype(F32)).astype(BF16)


def kernel(x, c, ctx, c_ctx, w_mod, b_mod, norm_mix, norm_ffn, w_in, conv_w, decay_w0, decay_w2, iclr_a0, iclr_a2, gate_g2, key_k, key_a, bonus_rho, gn_g, gn_b, q_norm, k_norm, w_branch_rwkv, w_branch_attn, w_out, w_router, w_gate, w_up, w_down, norm_final):
    B, T, D = x.shape
    Lc = ctx.shape[1]
    depth = w_in.shape[0]
    W = RWKV_WIDTH
    H = RWKV_HEADS
    tm = ROW_TILE
    assert Lc % tm == 0 and T % tm == 0 and tm % SCAN_CHUNK == 0 and B + 1 <= SUBLANES

    xs = jnp.concatenate([ctx, x], axis=1)
    Tc = Lc + T
    cvecs = jnp.zeros((SUBLANES, D), F32).at[:B].set(c).at[B].set(c_ctx)
    cos, sin = _rope_tables(T, Lc, ATTN_WIDTH)
    head = lax.broadcasted_iota(jnp.int32, (W, W), 0) // HEAD_DIM
    ones = (head == head.T).astype(BF16)
    tk = 3 * ROW_TILE if Tc % (3 * ROW_TILE) == 0 else ROW_TILE
    tq = tk
    inproj_tm = 3 * ROW_TILE if Tc % (3 * ROW_TILE) == 0 else ROW_TILE

    for l in range(depth):
        m = _mods(cvecs, w_mod[l], b_mod[l]).reshape(SUBLANES, 6, D)
        m = jnp.pad(m, ((0, 0), (0, SUBLANES - 6), (0, 0)))
        mods = jnp.stack([jnp.broadcast_to(m[B], (B, SUBLANES, D)), m[:B]], axis=1)

        U = _inproj(xs, mods, norm_mix[l], _pack_cols(w_in[l]).astype(BF16), Lc, inproj_tm)

        conv_packed = _pad_rows(_pack_cols(jnp.pad(conv_w[l], ((0, 0), (0, w_in.shape[2] - conv_w.shape[2]))))
                                [:, RW_OFF:RW_OFF + RW_COLS], SUBLANES)
        vecs = _pad_rows(jnp.stack([decay_w0[l, 0], decay_w0[l, 1], iclr_a0[l, 0], iclr_a0[l, 1],
                                    key_k[l], key_a[l], bonus_rho[l].reshape(W)]), SUBLANES)
        g2 = _split_param(_pad_rows(gate_g2[l], 256))
        dw2 = _split_param(jnp.stack([_pad_rows(decay_w2[l, z], 2 * LORA_DECAY, z * LORA_DECAY) for z in range(2)]))
        ia2 = _split_param(jnp.stack([_pad_rows(iclr_a2[l, z], 2 * LORA_ICLR, z * LORA_ICLR) for z in range(2)]))
        P, PC = _rwkv_prep(U, conv_packed, vecs, g2, dw2, ia2, ones, Lc, tm)
        Y = _scan(P, PC, Lc)

        qn = jnp.tile(q_norm[l], ATTN_WIDTH // HEAD_DIM).reshape(1, ATTN_WIDTH)
        kn = jnp.tile(k_norm[l], ATTN_WIDTH // HEAD_DIM).reshape(1, ATTN_WIDTH)
        q, k, v = _attn_prep(U, cos, sin, qn, kn, ones, tm)
        O = _flash(q, k, v, Lc, tq, tk)

        gn = jnp.stack([gn_g[l].reshape(H, HEAD_DIM), gn_b[l].reshape(H, HEAD_DIM)])
        rt = _split_param(jnp.pad(w_router[l], ((0, 0), (0, LANES - N_EXPERTS))))
        rtt = _split_param(w_router[l].T)
        x1, h2, aff, afft = _merge(xs, mods, Y, P, O, U, gn, norm_ffn[l], w_branch_rwkv[l].astype(BF16),
                                   w_branch_attn[l].astype(BF16), w_out[l].astype(BF16), rt, rtt, Lc, tm)

        last = l == depth - 1
        h_flat = h2.reshape(B * Tc, D)
        ye, idx, tab, cap = _moe_set(afft, h_flat, l, w_gate, w_up, w_down, Tc, Lc, T, tm)
        if last:
            xs = _combine(ye, idx, tab, x1, aff, mods, norm_final, Lc, T, cap, tm, False, True)
        else:
            xs = _combine(ye, idx, tab, x1, aff, mods, norm_final, 0, Tc, cap, tm, False, False)
            ye, idx, tab, cap = _moe_set(afft, h_flat, l, w_gate, w_up, w_down, Tc, 0, Lc, tm)
            xs = _combine(ye, idx, tab, x1, aff, mods, norm_final, 0, Lc, cap, tm, True, False, into=xs)
    return xs
```

```python
import functools
import math

import jax
import jax.numpy as jnp
from jax import lax
from jax.experimental import pallas as pl
from jax.experimental.pallas import tpu as pltpu

F32 = jnp.float32
BF16 = jnp.bfloat16

HEAD_DIM = 64
RWKV_HEADS = 8
RWKV_WIDTH = RWKV_HEADS * HEAD_DIM
LORA_DECAY = 64
LORA_ICLR = 64
LORA_GATE = 160
DECAY_SCALE = math.exp(-0.5)
GN_EPS = 64e-5
Q_HEADS = 8
KV_HEADS = 2
Q_PER_KV = Q_HEADS // KV_HEADS
ATTN_WIDTH = Q_HEADS * HEAD_DIM
KV_WIDTH = KV_HEADS * HEAD_DIM
ATTN_SCALE = HEAD_DIM ** -0.5
GRID_W = 64
ROPE_BASE = 10000.0
ROPE_PAIRS = HEAD_DIM // 4
N_EXPERTS = 16
EC_CAPACITY = 2
NORM_EPS = 1e-6

VMEM_LIMIT_BYTES = 56 * 1024 * 1024
LANES = 128
SUBLANES = 8

GATE_OFF = 0
GATE_COLS = 2048
RW_OFF = 2048
RW_COLS = 2048
RW_GLO = 1536
RW_DLO = 1792
RW_ALO = 1920
Q_OFF = 4096
KV_OFF = 4608
PACKED_COLS = 4864

SCAN_CHUNK = 64
ROW_TILE = 256
N_SLOTS = 11
FLASH_UNIT_ROWS = 256
SCAN_GROUP = 4
SCAN_SEQ = 4


def _cparams(*sem):
    return pltpu.CompilerParams(dimension_semantics=sem, vmem_limit_bytes=VMEM_LIMIT_BYTES)


def _split2(a):
    hi = a.astype(BF16)
    lo = (a - hi.astype(F32)).astype(BF16)
    return hi, lo


def _dot(a, b):
    return jnp.dot(a, b, preferred_element_type=F32)


def _dot_nt(a, b):
    return lax.dot_general(a, b, (((1,), (1,)), ((), ())), preferred_element_type=F32)


def _dot_tn(a, b):
    return lax.dot_general(a, b, (((0,), (0,)), ((), ())), preferred_element_type=F32)


def _dot3(a, b_hi, b_lo, dot=_dot):
    a_hi, a_lo = _split2(a)
    return dot(a_hi, b_hi) + dot(a_hi, b_lo) + dot(a_lo, b_hi)


def _dot3f(a, b, dot=_dot):
    b_hi, b_lo = _split2(b)
    return _dot3(a, b_hi, b_lo, dot)


def _segsum(a, ones_bf16):
    a_hi, a_lo = _split2(a)
    return _dot(a_hi, ones_bf16) + _dot(a_lo, ones_bf16)


def _sigmoid(x):
    return 1.0 / (1.0 + jnp.exp(-x))


def _select_mod(mod_ref, k, is_ctx):
    return jnp.where(is_ctx, mod_ref[0, k:k + 1, :], mod_ref[1, k:k + 1, :])


def _mods_kernel(c_ref, w_ref, b_ref, o_ref):
    cv = c_ref[...]
    s = cv * _sigmoid(cv)
    o_ref[...] = _dot3f(s, w_ref[...]) + b_ref[...]


def _mods(cvecs, w_mod, b_mod):
    D, N = w_mod.shape
    tn = N // 4
    return pl.pallas_call(
        _mods_kernel,
        grid=(N // tn,),
        in_specs=[pl.BlockSpec((SUBLANES, D), lambda j: (0, 0)),
                  pl.BlockSpec((D, tn), lambda j: (0, j)),
                  pl.BlockSpec((1, tn), lambda j: (0, j))],
        out_specs=pl.BlockSpec((SUBLANES, tn), lambda j: (0, j)),
        out_shape=jax.ShapeDtypeStruct((SUBLANES, N), F32),
        compiler_params=_cparams("arbitrary"),
    )(cvecs, w_mod, b_mod.reshape(1, N))


def _inproj_kernel(ctx_len, x_ref, mod_ref, nw_ref, w_ref, o_ref):
    tm = x_ref.shape[0]
    row = pl.program_id(2) * tm + lax.broadcasted_iota(jnp.int32, (tm, 1), 0)
    is_ctx = row < ctx_len
    x = x_ref[...]
    h = x * lax.rsqrt(jnp.mean(x * x, axis=-1, keepdims=True) + NORM_EPS) * nw_ref[...]
    h = h * (1.0 + _select_mod(mod_ref, 1, is_ctx)) + _select_mod(mod_ref, 0, is_ctx)
    o_ref[...] = _dot(h.astype(BF16), w_ref[...])


def _inproj(xs, mods, norm_w, w_packed, ctx_len, tm):
    B, Tc, D = xs.shape
    N = w_packed.shape[1]
    tn = N // 2
    return pl.pallas_call(
        functools.partial(_inproj_kernel, ctx_len),
        grid=(N // tn, B, Tc // tm),
        in_specs=[pl.BlockSpec((None, tm, D), lambda n, b, i: (b, i, 0)),
                  pl.BlockSpec((None, 2, SUBLANES, D), lambda n, b, i: (b, 0, 0, 0)),
                  pl.BlockSpec((1, D), lambda n, b, i: (0, 0)),
                  pl.BlockSpec((D, tn), lambda n, b, i: (0, n))],
        out_specs=pl.BlockSpec((None, tm, tn), lambda n, b, i: (b, i, n)),
        out_shape=jax.ShapeDtypeStruct((B, Tc, N), F32),
        compiler_params=_cparams("arbitrary", "arbitrary", "arbitrary"),
    )(xs, mods, norm_w.reshape(1, D), w_packed)


def _rwkv_prep_kernel(ctx_len, total_len, u_ref, up_ref, un_ref, cw_ref, vec_ref,
                      g2h_ref, g2l_ref, dwh_ref, dwl_ref, iah_ref, ial_ref, ones_ref,
                      p_ref, pc_ref):
    tm = u_ref.shape[0]
    C = SCAN_CHUNK
    W = RWKV_WIDTH
    t0 = pl.program_id(1) * tm
    prev_ok = jnp.logical_and(t0 != 0, t0 != ctx_len).astype(F32)
    next_ok = jnp.logical_and(t0 + tm != ctx_len, t0 + tm != total_len).astype(F32)
    row = lax.broadcasted_iota(jnp.int32, (tm, 1), 0)

    u = u_ref[...]
    prev_row = up_ref[SUBLANES - 1:SUBLANES, :] * prev_ok
    next_row = un_ref[0:1, :] * next_ok
    up = jnp.where(row == 0, prev_row, pltpu.roll(u, 1, 0))
    dn = jnp.where(row == tm - 1, next_row, pltpu.roll(u, tm - 1, 0))
    uc = up * cw_ref[0:1, :] + u * cw_ref[1:2, :] + dn * cw_ref[2:3, :]

    r = uc[:, 0:W]
    k = uc[:, W:2 * W]
    v = uc[:, 2 * W:3 * W]
    ones = ones_ref[...]
    key_k = vec_ref[4:5, :]
    key_a = vec_ref[5:6, :]
    rho = vec_ref[6:7, :]

    g = _dot3(_sigmoid(uc[:, RW_GLO:RW_GLO + 256]), g2h_ref[...], g2l_ref[...])
    dl = jnp.tanh(uc[:, RW_DLO:RW_DLO + 128])
    al = uc[:, RW_ALO:RW_ALO + 128]

    kk = k * key_k
    kk = kk * lax.rsqrt(_segsum(kk * kk, ones) + 1e-12)

    ti = lax.broadcasted_iota(jnp.int32, (tm, tm), 0)
    si = lax.broadcasted_iota(jnp.int32, (tm, tm), 1)
    same_chunk = (ti // C) == (si // C)
    tri = (jnp.logical_and(same_chunk, si <= ti).astype(BF16),
           jnp.logical_and(same_chunk, si >= ti).astype(BF16))

    streams = [v, g, None]
    krep_sum = None
    for z in range(2):
        d = vec_ref[z:z + 1, :] + _dot3(dl, dwh_ref[z], dwl_ref[z])
        logw = -DECAY_SCALE * _sigmoid(d)
        a = _sigmoid(vec_ref[2 + z:3 + z, :] + _dot3(al, iah_ref[z], ial_ref[z]))
        krep = k * (1.0 + (a - 1.0) * key_a)
        ka = kk * a
        krep_sum = krep if krep_sum is None else krep_sum + krep
        l_hi = logw.astype(BF16)
        rem = logw - l_hi.astype(F32)
        l_mid = rem.astype(BF16)
        l_lo = (rem - l_mid.astype(F32)).astype(BF16)
        cum = _dot(tri[z], l_hi) + _dot(tri[z], l_mid) + _dot(tri[z], l_lo)
        e_neg = jnp.exp(-cum)
        streams += [kk * jnp.exp(cum - logw), ka * e_neg, krep * e_neg, r * jnp.exp(cum)]
        for j in range(tm // C):
            last = j * C + (C - 1 if z == 0 else 0)
            pc = jnp.exp(cum[last:last + 1, :])
            for h in range(RWKV_HEADS):
                pc_ref[z, j, h:h + 1, :] = pc[:, h * HEAD_DIM:(h + 1) * HEAD_DIM]
    bonus = _segsum(r * krep_sum * rho, ones)
    streams[2] = bonus * v
    for s, val in enumerate(streams):
        for h in range(RWKV_HEADS):
            p_ref[s, h] = val[:, h * HEAD_DIM:(h + 1) * HEAD_DIM]


def _rwkv_prep(U, conv_packed, vecs, g2, dw2, ia2, ones, ctx_len, tm):
    B, Tc, _ = U.shape
    C = SCAN_CHUNK
    H = RWKV_HEADS
    rb = RW_OFF // RW_COLS
    nt = Tc // tm
    hb = tm // SUBLANES
    full = lambda a: pl.BlockSpec(a.shape, lambda b, i: (0,) * a.ndim)
    args = (conv_packed, vecs, g2[0], g2[1], dw2[0], dw2[1], ia2[0], ia2[1], ones)
    return pl.pallas_call(
        functools.partial(_rwkv_prep_kernel, ctx_len, Tc),
        grid=(B, nt),
        in_specs=[pl.BlockSpec((None, tm, RW_COLS), lambda b, i: (b, i, rb)),
                  pl.BlockSpec((None, SUBLANES, RW_COLS),
                               lambda b, i: (b, jnp.maximum(i * hb - 1, 0), rb)),
                  pl.BlockSpec((None, SUBLANES, RW_COLS),
                               lambda b, i: (b, jnp.minimum((i + 1) * hb, Tc // SUBLANES - 1), rb))]
                 + [full(a) for a in args],
        out_specs=[pl.BlockSpec((None, N_SLOTS, H, tm, HEAD_DIM), lambda b, i: (b, 0, 0, i, 0)),
                   pl.BlockSpec((None, 2, tm // C, H, HEAD_DIM), lambda b, i: (b, 0, i, 0, 0))],
        out_shape=[jax.ShapeDtypeStruct((B, N_SLOTS, H, Tc, HEAD_DIM), F32),
                   jax.ShapeDtypeStruct((B, 2, Tc // C, H, HEAD_DIM), F32)],
        compiler_params=_cparams("arbitrary", "arbitrary"),
    )(U, U, U, *args)


def _scan_local_kernel(v_ref, a_ref, b_ref, k_ref, r_ref, pc_ref, rb_ref, yb_ref, m_ref, hm_ref):
    C = SCAN_CHUNK
    rev = pl.program_id(1) == 1
    t2 = lax.broadcasted_iota(jnp.int32, (C, 2 * C), 0)
    s2 = lax.broadcasted_iota(jnp.int32, (C, 2 * C), 1)
    right = s2 >= C
    s2 = jnp.where(right, s2 - C, s2)
    d2 = jnp.where(rev, t2 - s2, s2 - t2)
    strict_right = jnp.logical_and(d2 < 0, right)
    incl2 = d2 <= 0
    t1 = lax.broadcasted_iota(jnp.int32, (C, C), 0)
    s1 = lax.broadcasted_iota(jnp.int32, (C, C), 1)
    strict1 = jnp.where(rev, t1 - s1, s1 - t1) < 0
    ident = (t1 == s1).astype(F32)

    units = [(c, h) for c in range(SCAN_GROUP) for h in range(RWKV_HEADS)]
    n = range(len(units))
    rows = lambda ref, c, h: ref[h, c * C:(c + 1) * C, :]
    V = [rows(v_ref, c, h) for c, h in units]
    A = [rows(a_ref, c, h) for c, h in units]
    Bm = [rows(b_ref, c, h) for c, h in units]
    R = [rows(r_ref, c, h) for c, h in units]
    AR = [jnp.concatenate([A[i], R[i]], axis=0).astype(BF16) for i in n]
    BK = [jnp.concatenate([Bm[i], rows(k_ref, c, h)], axis=0).astype(BF16) for i, (c, h) in enumerate(units)]
    pc = [pc_ref[c, h:h + 1, :] for c, h in units]
    mm = lambda a, b, dot=_dot: dot(a.astype(BF16), b.astype(BF16))
    G = [_dot_nt(AR[i], BK[i]) for i in n]
    n_ab = [jnp.where(strict1, G[i][:C, :C], 0.0) for i in n]
    n_ak_wide = [jnp.where(strict_right, G[i][:C], 0.0) for i in n]
    n_r_wide = [jnp.where(incl2, jnp.where(right, G[i][C:], -G[i][C:]), 0.0) for i in n]
    nv = [mm(n_ak_wide[i], jnp.concatenate([V[i], V[i]], axis=0)) for i in n]

    inv = [ident - n_ab[i] for i in n]
    power = n_ab
    for _ in range(5):
        pb = [power[i].astype(BF16) for i in n]
        power = [_dot(pb[i], pb[i]) for i in n]
        inv = [_dot(inv[i].astype(BF16), (ident + power[i]).astype(BF16)) for i in n]

    ab = [mm(inv[i], A[i]) for i in n]
    ub = [mm(inv[i], nv[i]) for i in n]
    for i, (c, h) in enumerate(units):
        rb_ref[h, c * C:(c + 1) * C, :] = R[i] + mm(n_r_wide[i][:, :C], ab[i])
    for i, (c, h) in enumerate(units):
        yb_ref[h, c * C:(c + 1) * C, :] = mm(n_r_wide[i], jnp.concatenate([ub[i], V[i]], axis=0))
    for i, (c, h) in enumerate(units):
        m_ref[c, h] = (ident - mm(ab[i], Bm[i], _dot_tn)) * pc[i]
    for i, (c, h) in enumerate(units):
        W = jnp.concatenate([-ub[i], V[i]], axis=0).astype(BF16)
        hm_ref[c, h] = _dot_tn(W, BK[i]) * pc[i]


def _scan_seq_kernel(rbf_ref, ybf_ref, mf_ref, hmf_ref, rbr_ref, ybr_ref, mr_ref, hmr_ref,
                     yf_ref, yr_ref, state_ref):
    C = SCAN_CHUNK

    @pl.when(pl.program_id(1) == 0)
    def _():
        state_ref[...] = jnp.zeros_like(state_ref)

    dirs = ((rbf_ref, ybf_ref, mf_ref, hmf_ref, yf_ref), (rbr_ref, ybr_ref, mr_ref, hmr_ref, yr_ref))
    chains = [(z, h) for z in range(2) for h in range(RWKV_HEADS)]
    S = [state_ref[z, h] for z, h in chains]
    for step in range(SCAN_SEQ):
        Ss = [_split2(s) for s in S]
        nxt = []
        for i, (z, h) in enumerate(chains):
            rb_ref, yb_ref, m_ref, hm_ref, y_ref = dirs[z]
            c = step if z == 0 else SCAN_SEQ - 1 - step
            rows = slice(c * C, (c + 1) * C)
            y_ref[h, rows, :] = _dot_nt(rb_ref[h, rows, :].astype(BF16), Ss[i][0]) + yb_ref[h, rows, :]
            m_hi, m_lo = _split2(m_ref[c, h])
            nxt.append(_dot(Ss[i][0], m_hi) + _dot(Ss[i][0], m_lo) + _dot(Ss[i][1], m_hi) + hm_ref[c, h])
        S = nxt
    for i, (z, h) in enumerate(chains):
        state_ref[z, h] = S[i]


def _scan(P, PC, ctx_len):
    B, _, H, Tc, _ = P.shape
    C = SCAN_CHUNK
    nch = Tc // C
    assert nch % SCAN_GROUP == 0 and nch % SCAN_SEQ == 0 and (ctx_len // C) % SCAN_SEQ == 0
    gc = SCAN_GROUP * C
    tok = lambda slot: pl.BlockSpec((None, None, H, gc, HEAD_DIM), lambda b, z, j: (b, slot(z), 0, j, 0))
    mat_shape = jax.ShapeDtypeStruct((2, B, nch, H, HEAD_DIM, HEAD_DIM), F32)
    tok_shape = jax.ShapeDtypeStruct((2, B, H, Tc, HEAD_DIM), F32)
    RB, YB, MM, HM = pl.pallas_call(
        _scan_local_kernel,
        grid=(B, 2, nch // SCAN_GROUP),
        in_specs=[tok(lambda z: 0), tok(lambda z: 3 + 4 * z), tok(lambda z: 4 + 4 * z), tok(lambda z: 5 + 4 * z),
                  tok(lambda z: 6 + 4 * z),
                  pl.BlockSpec((None, None, SCAN_GROUP, H, HEAD_DIM), lambda b, z, j: (b, z, j, 0, 0))],
        out_specs=[pl.BlockSpec((None, None, H, gc, HEAD_DIM), lambda b, z, j: (z, b, 0, j, 0)),
                   pl.BlockSpec((None, None, H, gc, HEAD_DIM), lambda b, z, j: (z, b, 0, j, 0)),
                   pl.BlockSpec((None, None, SCAN_GROUP, H, HEAD_DIM, HEAD_DIM), lambda b, z, j: (z, b, j, 0, 0, 0)),
                   pl.BlockSpec((None, None, SCAN_GROUP, H, HEAD_DIM, HEAD_DIM), lambda b, z, j: (z, b, j, 0, 0, 0))],
        out_shape=[tok_shape, tok_shape, mat_shape, mat_shape],
        compiler_params=_cparams("arbitrary", "arbitrary", "arbitrary"),
    )(P, P, P, P, P, PC)

    nblk = nch // SCAN_SEQ
    nctx = ctx_len // C // SCAN_SEQ

    def block(z, j):
        return j if z == 0 else jnp.where(j < nctx, nctx - 1 - j, nblk + nctx - 1 - j)

    sc = SCAN_SEQ * C
    tok_spec = lambda z: pl.BlockSpec((None, None, H, sc, HEAD_DIM), lambda b, j: (z, b, 0, block(z, j), 0))
    mat_spec = lambda z: pl.BlockSpec((None, None, SCAN_SEQ, H, HEAD_DIM, HEAD_DIM),
                                      lambda b, j: (z, b, block(z, j), 0, 0, 0))
    out_spec = lambda z: pl.BlockSpec((None, H, sc, HEAD_DIM), lambda b, j: (b, 0, block(z, j), 0))
    y_shape = jax.ShapeDtypeStruct((B, H, Tc, HEAD_DIM), F32)
    return pl.pallas_call(
        _scan_seq_kernel,
        grid=(B, nblk),
        in_specs=[tok_spec(0), tok_spec(0), mat_spec(0), mat_spec(0),
                  tok_spec(1), tok_spec(1), mat_spec(1), mat_spec(1)],
        out_specs=[out_spec(0), out_spec(1)],
        out_shape=[y_shape, y_shape],
        scratch_shapes=[pltpu.VMEM((2, H, HEAD_DIM, HEAD_DIM), F32)],
        compiler_params=_cparams("arbitrary", "arbitrary"),
    )(RB, YB, MM, HM, RB, YB, MM, HM)


def _attn_prep_kernel(uq_ref, ukv_ref, cos_ref, sin_ref, qn_ref, kn_ref, ones_ref, q_ref, k_ref, v_ref):
    ones = ones_ref[...]
    cos = cos_ref[...]
    sin = sin_ref[...]
    lane = lax.broadcasted_iota(jnp.int32, (1, ATTN_WIDTH), 1)
    first_half = (lane % (2 * ROPE_PAIRS)) < ROPE_PAIRS

    def norm_rope(x, gain, width):
        ms = _segsum(x * x, ones[:width, :width]) * (1.0 / HEAD_DIM)
        y = x * lax.rsqrt(ms + NORM_EPS) * gain
        partner = jnp.where(first_half[:, :width],
                            pltpu.roll(y, width - ROPE_PAIRS, 1), pltpu.roll(y, ROPE_PAIRS, 1))
        return y * cos[:, :width] + partner * sin[:, :width]

    q = norm_rope(uq_ref[...], qn_ref[...], ATTN_WIDTH) * (ATTN_SCALE * math.log2(math.e))
    ukv = ukv_ref[...]
    k = norm_rope(ukv[:, :KV_WIDTH], kn_ref[:, :KV_WIDTH], KV_WIDTH)
    v = ukv[:, KV_WIDTH:]
    low_lanes = lax.broadcasted_iota(jnp.int32, (1, KV_WIDTH), 1) < HEAD_DIM
    for g in range(KV_HEADS):
        for j in range(Q_PER_KV):
            o = (g * Q_PER_KV + j) * HEAD_DIM
            q_ref[g, j] = q[:, o:o + HEAD_DIM].astype(BF16)
        k_ref[g] = k[:, g * HEAD_DIM:(g + 1) * HEAD_DIM].astype(BF16)
        v_g = v if g == 0 else pltpu.roll(v, (KV_HEADS - g) * HEAD_DIM, 1)
        v_ref[g] = jnp.where(low_lanes, v_g, 1.0).astype(BF16)


def _attn_prep(U, cos, sin, qn, kn, ones, tm):
    B, Tc, _ = U.shape
    full = lambda a: pl.BlockSpec(a.shape, lambda b, i: (0,) * a.ndim)
    return pl.pallas_call(
        _attn_prep_kernel,
        grid=(B, Tc // tm),
        in_specs=[pl.BlockSpec((None, tm, ATTN_WIDTH), lambda b, i: (b, i, Q_OFF // ATTN_WIDTH)),
                  pl.BlockSpec((None, tm, 2 * KV_WIDTH), lambda b, i: (b, i, KV_OFF // (2 * KV_WIDTH))),
                  pl.BlockSpec((tm, ATTN_WIDTH), lambda b, i: (i, 0)),
                  pl.BlockSpec((tm, ATTN_WIDTH), lambda b, i: (i, 0)),
                  full(qn), full(kn), full(ones)],
        out_specs=[pl.BlockSpec((None, KV_HEADS, Q_PER_KV, tm, HEAD_DIM), lambda b, i: (b, 0, 0, i, 0)),
                   pl.BlockSpec((None, KV_HEADS, tm, HEAD_DIM), lambda b, i: (b, 0, i, 0)),
                   pl.BlockSpec((None, KV_HEADS, tm, 2 * HEAD_DIM), lambda b, i: (b, 0, i, 0))],
        out_shape=[jax.ShapeDtypeStruct((B, KV_HEADS, Q_PER_KV, Tc, HEAD_DIM), BF16),
                   jax.ShapeDtypeStruct((B, KV_HEADS, Tc, HEAD_DIM), BF16),
                   jax.ShapeDtypeStruct((B, KV_HEADS, Tc, 2 * HEAD_DIM), BF16)],
        compiler_params=_cparams("arbitrary", "arbitrary"),
    )(U, U, cos, sin, qn, kn, ones)


def _flash_kernel(ctx_len, q_ref, k_ref, v_ref, o_ref, m_ref, acc_ref):
    G, tq, _ = q_ref.shape
    tk = k_ref.shape[0]
    qi = pl.program_id(2)
    ki = pl.program_id(3)
    has_ctx_rows = qi * tq < ctx_len

    @pl.when(ki == 0)
    def _():
        m_ref[...] = jnp.full_like(m_ref, -jnp.inf)
        acc_ref[...] = jnp.zeros_like(acc_ref)

    def step(masked):
        k = k_ref[...]
        v = v_ref[...]
        rows = FLASH_UNIT_ROWS
        units = [(j, r0) for j in range(G) for r0 in range(0, tq, rows)]

        def logits(u):
            j, r0 = u
            s = _dot_nt(q_ref[j, r0:r0 + rows, :], k)
            if masked:
                key = ki * tk + lax.broadcasted_iota(jnp.int32, (1, tk), 1)
                row = qi * tq + r0 + lax.broadcasted_iota(jnp.int32, (rows, 1), 0)
                s = jnp.where(jnp.logical_or(row >= ctx_len, key < ctx_len), s, -jnp.inf)
            return s

        s_next = logits(units[0])
        for n, (j, r0) in enumerate(units):
            s = s_next
            if n + 1 < len(units):
                s_next = logits(units[n + 1])
            m_old = m_ref[j, r0:r0 + rows, :]
            m_new = jnp.maximum(m_old, jnp.max(s, axis=-1, keepdims=True))
            p = jnp.exp2(s - m_new).astype(BF16)
            acc_ref[j, r0:r0 + rows, :] = jnp.exp2(m_old - m_new) * acc_ref[j, r0:r0 + rows, :] + _dot(p, v)
            m_ref[j, r0:r0 + rows, :] = m_new

    @pl.when(jnp.logical_not(has_ctx_rows))
    def _():
        step(False)

    @pl.when(has_ctx_rows)
    def _():
        step(True)

    @pl.when(ki == pl.num_programs(3) - 1)
    def _():
        for j in range(G):
            acc = acc_ref[j]
            o_ref[j] = acc[:, :HEAD_DIM] / acc[:, HEAD_DIM:]


def _flash(q, k, v, ctx_len, tq, tk):
    B, KVH, G, Tc, Dh = q.shape
    return pl.pallas_call(
        functools.partial(_flash_kernel, ctx_len),
        grid=(B, KVH, Tc // tq, Tc // tk),
        in_specs=[pl.BlockSpec((None, None, G, tq, Dh), lambda b, g, i, j: (b, g, 0, i, 0)),
                  pl.BlockSpec((None, None, tk, Dh), lambda b, g, i, j: (b, g, j, 0)),
                  pl.BlockSpec((None, None, tk, 2 * Dh), lambda b, g, i, j: (b, g, j, 0))],
        out_specs=pl.BlockSpec((None, None, G, tq, Dh), lambda b, g, i, j: (b, g, 0, i, 0)),
        out_shape=jax.ShapeDtypeStruct((B, KVH, G, Tc, Dh), F32),
        scratch_shapes=[pltpu.VMEM((G, tq, 1), F32), pltpu.VMEM((G, tq, 2 * Dh), F32)],
        compiler_params=_cparams("arbitrary", "arbitrary", "arbitrary", "arbitrary"),
    )(q, k, v)


def _merge_kernel(ctx_len, x_ref, mod_ref, yf_ref, yr_ref, g_ref, bv_ref, o_ref, gate_ref, gn_ref, nw_ref,
                  wr_ref, wa_ref, wo_ref, rth_ref, rtl_ref, rtth_ref, rttl_ref,
                  x1_ref, h2_ref, aff_ref, afft_ref):
    tm = x_ref.shape[0]
    D = x_ref.shape[1]
    row = pl.program_id(1) * tm + lax.broadcasted_iota(jnp.int32, (tm, 1), 0)
    is_ctx = row < ctx_len

    rw = []
    for h in range(RWKV_HEADS):
        y = yf_ref[h] + yr_ref[h]
        mu = jnp.mean(y, axis=-1, keepdims=True)
        yc = y - mu
        var = jnp.mean(yc * yc, axis=-1, keepdims=True)
        yn = yc * lax.rsqrt(var + GN_EPS) * gn_ref[0, h:h + 1, :] + gn_ref[1, h:h + 1, :]
        rw.append((yn + bv_ref[h]) * g_ref[h])
    y_rwkv = jnp.concatenate(rw, axis=-1).astype(BF16)
    y_attn = jnp.concatenate([o_ref[g, j] for g in range(KV_HEADS) for j in range(Q_PER_KV)],
                             axis=-1).astype(BF16)
    gates = gate_ref[...]
    merged = (_sigmoid(gates[:, :D]) * _dot(y_rwkv, wr_ref[...])
              + _sigmoid(gates[:, D:]) * _dot(y_attn, wa_ref[...]))
    x1 = x_ref[...] + _select_mod(mod_ref, 2, is_ctx) * _dot(merged.astype(BF16), wo_ref[...])
    x1_ref[...] = x1

    h2 = x1 * lax.rsqrt(jnp.mean(x1 * x1, axis=-1, keepdims=True) + NORM_EPS) * nw_ref[...]
    h2 = h2 * (1.0 + _select_mod(mod_ref, 4, is_ctx)) + _select_mod(mod_ref, 3, is_ctx)
    h2_ref[...] = h2
    h_hi, h_lo = _split2(h2)
    rt_hi, rt_lo = rth_ref[...], rtl_ref[...]
    logits = _dot(h_hi, rt_hi) + _dot(h_hi, rt_lo) + _dot(h_lo, rt_hi)
    lane = lax.broadcasted_iota(jnp.int32, (1, LANES), 1)
    logits = jnp.where(lane < N_EXPERTS, logits, -jnp.inf)
    e = jnp.exp(logits - jnp.max(logits, axis=-1, keepdims=True))
    aff_ref[...] = e / jnp.sum(e, axis=-1, keepdims=True)
    rtt_hi, rtt_lo = rtth_ref[...], rttl_ref[...]
    logits_t = _dot_nt(rtt_hi, h_hi) + _dot_nt(rtt_hi, h_lo) + _dot_nt(rtt_lo, h_hi)
    et = jnp.exp(logits_t - jnp.max(logits_t, axis=0, keepdims=True))
    afft_ref[...] = et / jnp.sum(et, axis=0, keepdims=True)


def _merge(xs, mods, Y, P, O, U, gn, norm_w, wr, wa, wo, rt, rtt, ctx_len, tm):
    B, Tc, D = xs.shape
    H = RWKV_HEADS
    full = lambda a: pl.BlockSpec(a.shape, lambda b, i: (0,) * a.ndim)
    nw = norm_w.reshape(1, D)
    return pl.pallas_call(
        functools.partial(_merge_kernel, ctx_len),
        grid=(B, Tc // tm),
        in_specs=[pl.BlockSpec((None, tm, D), lambda b, i: (b, i, 0)),
                  pl.BlockSpec((None, 2, SUBLANES, D), lambda b, i: (b, 0, 0, 0)),
                  pl.BlockSpec((None, H, tm, HEAD_DIM), lambda b, i: (b, 0, i, 0)),
                  pl.BlockSpec((None, H, tm, HEAD_DIM), lambda b, i: (b, 0, i, 0)),
                  pl.BlockSpec((None, None, H, tm, HEAD_DIM), lambda b, i: (b, 1, 0, i, 0)),
                  pl.BlockSpec((None, None, H, tm, HEAD_DIM), lambda b, i: (b, 2, 0, i, 0)),
                  pl.BlockSpec((None, KV_HEADS, Q_PER_KV, tm, HEAD_DIM), lambda b, i: (b, 0, 0, i, 0)),
                  pl.BlockSpec((None, tm, GATE_COLS), lambda b, i: (b, i, GATE_OFF // GATE_COLS)),
                  full(gn), full(nw), full(wr), full(wa), full(wo), full(rt[0]), full(rt[1]),
                  full(rtt[0]), full(rtt[1])],
        out_specs=[pl.BlockSpec((None, tm, D), lambda b, i: (b, i, 0)),
                   pl.BlockSpec((None, tm, D), lambda b, i: (b, i, 0)),
                   pl.BlockSpec((None, tm, LANES), lambda b, i: (b, i, 0)),
                   pl.BlockSpec((None, N_EXPERTS, tm), lambda b, i: (b, 0, i))],
        out_shape=[jax.ShapeDtypeStruct((B, Tc, D), F32),
                   jax.ShapeDtypeStruct((B, Tc, D), F32),
                   jax.ShapeDtypeStruct((B, Tc, LANES), F32),
                   jax.ShapeDtypeStruct((B, N_EXPERTS, Tc), F32)],
        compiler_params=_cparams("arbitrary", "arbitrary"),
    )(xs, mods, Y[0], Y[1], P, P, O, U, gn, nw, wr, wa, wo, rt[0], rt[1], rtt[0], rtt[1])


def _route_kernel(start, n, cap, tile, afft_ref, idx_ref, tab_ref, bits_ref, cnt_ref, sel_ref):
    E, Tc = afft_ref.shape
    nblk = (start + n) // LANES
    tok = lax.broadcasted_iota(jnp.int32, (1, Tc), 1)
    in_set = jnp.logical_and(tok >= start, tok < start + n)
    bits_ref[...] = jnp.where(in_set, lax.bitcast_convert_type(afft_ref[...], jnp.int32), -1)

    def search(i, thr):
        cand = jnp.bitwise_or(thr, jnp.left_shift(1, 30 - i))
        count = jnp.sum(jnp.where(bits_ref[...] >= cand, 1.0, 0.0), axis=1, keepdims=True)
        return jnp.where(count >= cap, cand, thr)

    thr = lax.fori_loop(0, 31, search, jnp.zeros((E, 1), jnp.int32))
    above = jnp.sum(jnp.where(bits_ref[...] > thr, 1.0, 0.0), axis=1, keepdims=True)
    need = cap - above

    r = lax.broadcasted_iota(jnp.int32, (LANES, LANES), 0)
    c = lax.broadcasted_iota(jnp.int32, (LANES, LANES), 1)
    tri = (r <= c).astype(BF16)
    ties_before = jnp.zeros((E, 1), F32)
    taken_before = jnp.zeros((E, 1), F32)
    for j in range(nblk):
        blk = bits_ref[:, j * LANES:(j + 1) * LANES]
        tie = jnp.where(blk == thr, 1.0, 0.0)
        tie_rank = ties_before + _dot(tie.astype(BF16), tri) - tie
        sel = jnp.where(jnp.logical_or(blk > thr, jnp.logical_and(blk == thr, tie_rank < need)), 1.0, 0.0)
        cnt_ref[:, j * LANES:(j + 1) * LANES] = taken_before + _dot(sel.astype(BF16), tri)
        sel_ref[:, j * LANES:(j + 1) * LANES] = sel
        ties_before = ties_before + jnp.sum(tie, axis=1, keepdims=True)
        taken_before = taken_before + jnp.sum(sel, axis=1, keepdims=True)
    if nblk * LANES < Tc:
        cnt_ref[:, nblk * LANES:] = jnp.full((E, Tc - nblk * LANES), float(cap), F32)
        sel_ref[:, nblk * LANES:] = jnp.zeros((E, Tc - nblk * LANES), F32)

    t = lax.broadcasted_iota(jnp.int32, (Tc, LANES), 0)
    k = lax.broadcasted_iota(jnp.int32, (Tc, LANES), 1)
    tab_ref[...] = _dot(sel_ref[...].astype(BF16), (t < k * tile).astype(BF16)).astype(jnp.int32)

    sc = min(cap, 256)
    slot = lax.broadcasted_iota(jnp.int32, (sc, 1), 0).astype(F32)

    for e in range(E):
        for ci in range(cap // sc):
            p_col = slot + float(ci * sc)

            def per_block(j, acc, e=e, p_col=p_col):
                off = pl.multiple_of(j * LANES, LANES)
                return acc + jnp.where(cnt_ref[e:e + 1, pl.ds(off, LANES)] <= p_col, 1.0, 0.0)

            acc = lax.fori_loop(0, nblk, per_block, jnp.zeros((sc, LANES), F32))
            idx_ref[e, ci * sc:(ci + 1) * sc, :] = jnp.sum(acc, axis=1, keepdims=True).astype(jnp.int32)


def _route(afft, start, n, cap, tile):
    B, E, Tc = afft.shape
    return pl.pallas_call(
        functools.partial(_route_kernel, start, n, cap, tile),
        grid=(B,),
        in_specs=[pl.BlockSpec((None, E, Tc), lambda b: (b, 0, 0))],
        out_specs=[pl.BlockSpec((None, E, cap, 1), lambda b: (b, 0, 0, 0)),
                   pl.BlockSpec((None, E, LANES), lambda b: (b, 0, 0))],
        out_shape=[jax.ShapeDtypeStruct((B, E, cap, 1), jnp.int32),
                   jax.ShapeDtypeStruct((B, E, LANES), jnp.int32)],
        scratch_shapes=[pltpu.VMEM((E, Tc), jnp.int32), pltpu.VMEM((E, Tc), F32), pltpu.VMEM((E, Tc), F32)],
        compiler_params=_cparams("arbitrary"),
    )(afft)


def _ffn_kernel(n_steps, rows_ref, h_hbm, wg_ref, wu_ref, wd_ref, o_ref, xf_ref, xb_ref, acc_ref, sem):
    f = pl.program_id(2)
    tm = xb_ref.shape[0]
    tile = pl.program_id(0) * pl.num_programs(1) + pl.program_id(1)
    last_tile = pl.num_programs(0) * pl.num_programs(1) - 1
    slot = tile % 2

    def row_copy(buf, r, src_row):
        return pltpu.make_async_copy(h_hbm.at[pl.ds(src_row, 1)], xf_ref.at[buf, pl.ds(r, 1)], sem.at[buf])

    @pl.when(jnp.logical_and(tile == 0, f == 0))
    def _():
        def start(r, carry):
            row_copy(0, r, rows_ref[r]).start()
            return carry
        lax.fori_loop(0, tm, start, 0)

    next_base = jnp.minimum(tile + 1, last_tile) * tm
    per_step = tm // n_steps

    @pl.when(f == 0)
    def _():
        for r in range(tm):
            row_copy(slot, r, 0).wait()
        xb_ref[...] = xf_ref[slot].astype(BF16)
        acc_ref[...] = jnp.zeros_like(acc_ref)
        for r in range(per_step * n_steps, tm):
            row_copy(1 - slot, r, rows_ref[next_base + r]).start()

    for i in range(per_step):
        r = i * n_steps + f
        row_copy(1 - slot, r, rows_ref[next_base + r]).start()

    x = xb_ref[...]
    a = _dot(x, wg_ref[...].astype(BF16))
    b = _dot(x, wu_ref[...].astype(BF16))
    hid = (a * _sigmoid(a) * b).astype(BF16)
    acc_ref[...] += _dot(hid, wd_ref[...].astype(BF16))

    @pl.when(f == n_steps - 1)
    def _():
        o_ref[...] = acc_ref[...]

    @pl.when(jnp.logical_and(tile == last_tile, f == n_steps - 1))
    def _():
        for r in range(tm):
            row_copy(1 - slot, r, 0).wait()


def _expert_ffn(rows, h_flat, layer, w_gate, w_up, w_down, tm, tf):
    _, E, D, F = w_gate.shape
    M = rows.shape[0] // E
    n_steps = F // tf
    grid_spec = pltpu.PrefetchScalarGridSpec(
        num_scalar_prefetch=1,
        grid=(E, M // tm, n_steps),
        in_specs=[pl.BlockSpec(memory_space=pl.ANY),
                  pl.BlockSpec((None, None, D, tf), lambda e, i, f, rows: (layer, e, 0, f)),
                  pl.BlockSpec((None, None, D, tf), lambda e, i, f, rows: (layer, e, 0, f)),
                  pl.BlockSpec((None, None, tf, D), lambda e, i, f, rows: (layer, e, f, 0))],
        out_specs=pl.BlockSpec((None, tm, D), lambda e, i, f, rows: (e, i, 0)),
        scratch_shapes=[pltpu.VMEM((2, tm, D), F32), pltpu.VMEM((tm, D), BF16), pltpu.VMEM((tm, D), F32),
                        pltpu.SemaphoreType.DMA((2,))])
    return pl.pallas_call(
        functools.partial(_ffn_kernel, n_steps),
        grid_spec=grid_spec,
        out_shape=jax.ShapeDtypeStruct((E, M, D), F32),
        compiler_params=_cparams("arbitrary", "arbitrary", "arbitrary"),
    )(rows, h_flat, w_gate, w_up, w_down)


def _combine_kernel(set_start, cap, is_ctx, final, idx_ref, tab_ref, ye_hbm, x1_ref, aff_ref, mod_ref, nw_ref,
                    *rest):
    o_ref, stage_ref, sem = rest[-3:]
    E = N_EXPERTS
    tm = x1_ref.shape[0]
    b = pl.program_id(0)
    k = pl.program_id(1) + set_start // tm
    t0 = k * tm
    stage_ref[...] = jnp.zeros_like(stage_ref)

    def row_copy(e, s):
        tok = idx_ref[(b * E + e) * cap + s] - t0
        return pltpu.make_async_copy(ye_hbm.at[e, pl.ds(b * cap + s, 1)], stage_ref.at[e, pl.ds(tok, 1)], sem)

    bounds = [(tab_ref[(b * E + e) * LANES + k], tab_ref[(b * E + e) * LANES + k + 1]) for e in range(E)]
    for e in range(E):
        def start(s, carry, e=e):
            row_copy(e, s).start()
            return carry
        lax.fori_loop(bounds[e][0], bounds[e][1], start, 0)
    for e in range(E):
        count = bounds[e][1] - bounds[e][0]
        bulk = pl.multiple_of((count // SUBLANES) * SUBLANES, SUBLANES)

        @pl.when(bulk > 0)
        def _(e=e, bulk=bulk):
            pltpu.make_async_copy(ye_hbm.at[e, pl.ds(0, bulk)], stage_ref.at[e, pl.ds(0, bulk)], sem).wait()

        def wait(s, carry, e=e):
            row_copy(e, s).wait()
            return carry
        lax.fori_loop(bounds[e][0] + bulk, bounds[e][1], wait, 0)

    aff = aff_ref[...]
    total = stage_ref[0] * aff[:, 0:1]
    for e in range(1, E):
        total = total + stage_ref[e] * aff[:, e:e + 1]
    x = x1_ref[...] + mod_ref[0 if is_ctx else 1, 5:6, :] * total
    if final:
        x = x * lax.rsqrt(jnp.mean(x * x, axis=-1, keepdims=True) + NORM_EPS) * nw_ref[...]
    o_ref[...] = x


def _combine(ye, idx, tab, x1, aff, mods, norm_w, set_start, n, cap, tm, is_ctx, final, into=None):
    B, Tc, D = x1.shape
    E = N_EXPERTS
    off = set_start // tm
    in_specs = [pl.BlockSpec(memory_space=pl.ANY),
                pl.BlockSpec((None, tm, D), lambda b, i, *_: (b, i + off, 0)),
                pl.BlockSpec((None, tm, LANES), lambda b, i, *_: (b, i + off, 0)),
                pl.BlockSpec((None, 2, SUBLANES, D), lambda b, i, *_: (b, 0, 0, 0)),
                pl.BlockSpec((1, D), lambda b, i, *_: (0, 0))]
    args = [idx.reshape(-1), tab.reshape(-1), ye, x1, aff, mods, norm_w.reshape(1, D)]
    aliases = {}
    if into is not None:
        in_specs.append(pl.BlockSpec(memory_space=pl.ANY))
        aliases = {len(args): 0}
        args.append(into)
    if final:
        out_shape = jax.ShapeDtypeStruct((B, n, D), F32)
        out_spec = pl.BlockSpec((None, tm, D), lambda b, i, *_: (b, i, 0))
    else:
        out_shape = jax.ShapeDtypeStruct((B, Tc, D), F32)
        out_spec = pl.BlockSpec((None, tm, D), lambda b, i, *_: (b, i + off, 0))
    grid_spec = pltpu.PrefetchScalarGridSpec(
        num_scalar_prefetch=2, grid=(B, n // tm), in_specs=in_specs, out_specs=out_spec,
        scratch_shapes=[pltpu.VMEM((E, tm, D), F32), pltpu.SemaphoreType.DMA(())])
    return pl.pallas_call(
        functools.partial(_combine_kernel, set_start, cap, is_ctx, final),
        grid_spec=grid_spec,
        out_shape=out_shape,
        input_output_aliases=aliases,
        compiler_params=_cparams("arbitrary", "arbitrary"),
    )(*args)


def _moe_set(afft, h_flat, layer, w_gate, w_up, w_down, Tc, start, n, tile):
    B = afft.shape[0]
    E = N_EXPERTS
    cap = EC_CAPACITY * n // E
    idx, tab = _route(afft, start, n, cap, tile)
    idx = idx.reshape(B, E, cap)
    rows = idx + (jnp.arange(B, dtype=jnp.int32) * Tc)[:, None, None]
    rows = jnp.swapaxes(rows, 0, 1).reshape(-1)
    ye = _expert_ffn(rows, h_flat, layer, w_gate, w_up, w_down, min(1024, B * cap), 256)
    return ye, idx, tab, cap


def _pack_cols(w):
    W = RWKV_WIDTH
    lead = w.shape[:-1]
    zeros = lambda n: jnp.zeros(lead + (n,), w.dtype)
    rk = 3 * W
    rw_end = rk + LORA_GATE + 2 * LORA_DECAY + 2 * LORA_ICLR
    gqa_end = rw_end + ATTN_WIDTH + 2 * KV_WIDTH
    return jnp.concatenate([
        w[..., gqa_end:],
        w[..., :rk],
        w[..., rk:rk + LORA_GATE], zeros(256 - LORA_GATE),
        w[..., rk + LORA_GATE:rw_end],
        w[..., rw_end:gqa_end],
    ], axis=-1)


def _rope_tables(n_tokens, ctx_len, width):
    rows = n_tokens // GRID_W
    row = jnp.repeat(jnp.arange(rows, dtype=F32), GRID_W)
    col = jnp.broadcast_to(jnp.arange(GRID_W, dtype=F32)[None, :], (rows, GRID_W)).reshape(-1)
    inv = jnp.power(ROPE_BASE, -jnp.arange(ROPE_PAIRS, dtype=F32) / ROPE_PAIRS)
    ang = jnp.stack([row[:, None] * inv, col[:, None] * inv], axis=1)
    cos = jnp.repeat(jnp.cos(ang)[:, :, None, :], 2, axis=2).reshape(n_tokens, HEAD_DIM)
    sin = jnp.sin(ang)
    sin = jnp.stack([-sin, sin], axis=2).reshape(n_tokens, HEAD_DIM)
    cos = jnp.concatenate([jnp.ones((ctx_len, HEAD_DIM), F32), cos], axis=0)
    sin = jnp.concatenate([jnp.zeros((ctx_len, HEAD_DIM), F32), sin], axis=0)
    reps = width // HEAD_DIM
    return jnp.tile(cos, (1, reps)), jnp.tile(sin, (1, reps))


def _pad_rows(w, rows, offset=0):
    out = jnp.zeros((rows,) + w.shape[1:], w.dtype)
    return out.at[offset:offset + w.shape[0]].set(w)


def _split_param(w):
    hi = w.astype(BF16)
    return hi, (w - hi.astype(F32)).astype(BF16)


def kernel(x, c, ctx, c_ctx, w_mod, b_mod, norm_mix, norm_ffn, w_in, conv_w, decay_w0, decay_w2, iclr_a0, iclr_a2, gate_g2, key_k, key_a, bonus_rho, gn_g, gn_b, q_norm, k_norm, w_branch_rwkv, w_branch_attn, w_out, w_router, w_gate, w_up, w_down, norm_final):
    B, T, D = x.shape
    Lc = ctx.shape[1]
    depth = w_in.shape[0]
    W = RWKV_WIDTH
    H = RWKV_HEADS
    tm = ROW_TILE
    assert Lc % tm == 0 and T % tm == 0 and tm % SCAN_CHUNK == 0 and B + 1 <= SUBLANES

    xs = jnp.concatenate([ctx, x], axis=1)
    Tc = Lc + T
    cvecs = jnp.zeros((SUBLANES, D), F32).at[:B].set(c).at[B].set(c_ctx)
    cos, sin = _rope_tables(T, Lc, ATTN_WIDTH)
    head = lax.broadcasted_iota(jnp.int32, (W, W), 0) // HEAD_DIM
    ones = (head == head.T).astype(BF16)
    tk = 3 * ROW_TILE if Tc % (3 * ROW_TILE) == 0 else ROW_TILE
    tq = tk
    inproj_tm = 3 * ROW_TILE if Tc % (3 * ROW_TILE) == 0 else ROW_TILE

    for l in range(depth):
        m = _mods(cvecs, w_mod[l], b_mod[l]).reshape(SUBLANES, 6, D)
        m = jnp.pad(m, ((0, 0), (0, SUBLANES - 6), (0, 0)))
        mods = jnp.stack([jnp.broadcast_to(m[B], (B, SUBLANES, D)), m[:B]], axis=1)

        U = _inproj(xs, mods, norm_mix[l], _pack_cols(w_in[l]).astype(BF16), Lc, inproj_tm)

        conv_packed = _pad_rows(_pack_cols(jnp.pad(conv_w[l], ((0, 0), (0, w_in.shape[2] - conv_w.shape[2]))))
                                [:, RW_OFF:RW_OFF + RW_COLS], SUBLANES)
        vecs = _pad_rows(jnp.stack([decay_w0[l, 0], decay_w0[l, 1], iclr_a0[l, 0], iclr_a0[l, 1],
                                    key_k[l], key_a[l], bonus_rho[l].reshape(W)]), SUBLANES)
        g2 = _split_param(_pad_rows(gate_g2[l], 256))
        dw2 = _split_param(jnp.stack([_pad_rows(decay_w2[l, z], 2 * LORA_DECAY, z * LORA_DECAY) for z in range(2)]))
        ia2 = _split_param(jnp.stack([_pad_rows(iclr_a2[l, z], 2 * LORA_ICLR, z * LORA_ICLR) for z in range(2)]))
        P, PC = _rwkv_prep(U, conv_packed, vecs, g2, dw2, ia2, ones, Lc, tm)
        Y = _scan(P, PC, Lc)

        qn = jnp.tile(q_norm[l], ATTN_WIDTH // HEAD_DIM).reshape(1, ATTN_WIDTH)
        kn = jnp.tile(k_norm[l], ATTN_WIDTH // HEAD_DIM).reshape(1, ATTN_WIDTH)
        q, k, v = _attn_prep(U, cos, sin, qn, kn, ones, tm)
        O = _flash(q, k, v, Lc, tq, tk)

        gn = jnp.stack([gn_g[l].reshape(H, HEAD_DIM), gn_b[l].reshape(H, HEAD_DIM)])
        rt = _split_param(jnp.pad(w_router[l], ((0, 0), (0, LANES - N_EXPERTS))))
        rtt = _split_param(w_router[l].T)
        x1, h2, aff, afft = _merge(xs, mods, Y, P, O, U, gn, norm_ffn[l], w_branch_rwkv[l].astype(BF16),
                                   w_branch_attn[l].astype(BF16), w_out[l].astype(BF16), rt, rtt, Lc, tm)

        last = l == depth - 1
        h_flat = h2.reshape(B * Tc, D)
        ye, idx, tab, cap = _moe_set(afft, h_flat, l, w_gate, w_up, w_down, Tc, Lc, T, tm)
        if last:
            xs = _combine(ye, idx, tab, x1, aff, mods, norm_final, Lc, T, cap, tm, False, True)
        else:
            xs = _combine(ye, idx, tab, x1, aff, mods, norm_final, 0, Tc, cap, tm, False, False)
            ye, idx, tab, cap = _moe_set(afft, h_flat, l, w_gate, w_up, w_down, Tc, 0, Lc, tm)
            xs = _combine(ye, idx, tab, x1, aff, mods, norm_final, 0, Lc, cap, tm, True, False, into=xs)
    return xs
```

```python
import functools
import math

import jax
import jax.numpy as jnp
from jax import lax
from jax.experimental import pallas as pl
from jax.experimental.pallas import tpu as pltpu

F32 = jnp.float32
BF16 = jnp.bfloat16

HEAD_DIM = 64
RWKV_HEADS = 8
RWKV_WIDTH = RWKV_HEADS * HEAD_DIM
LORA_DECAY = 64
LORA_ICLR = 64
LORA_GATE = 160
DECAY_SCALE = math.exp(-0.5)
GN_EPS = 64e-5
Q_HEADS = 8
KV_HEADS = 2
Q_PER_KV = Q_HEADS // KV_HEADS
ATTN_WIDTH = Q_HEADS * HEAD_DIM
KV_WIDTH = KV_HEADS * HEAD_DIM
ATTN_SCALE = HEAD_DIM ** -0.5
GRID_W = 64
ROPE_BASE = 10000.0
ROPE_PAIRS = HEAD_DIM // 4
N_EXPERTS = 16
EC_CAPACITY = 2
NORM_EPS = 1e-6

VMEM_LIMIT_BYTES = 56 * 1024 * 1024
LANES = 128
SUBLANES = 8

GATE_OFF = 0
GATE_COLS = 2048
RW_OFF = 2048
RW_COLS = 2048
RW_GLO = 1536
RW_DLO = 1792
RW_ALO = 1920
Q_OFF = 4096
KV_OFF = 4608
PACKED_COLS = 4864

SCAN_CHUNK = 64
ROW_TILE = 256
N_SLOTS = 11
FLASH_UNIT_ROWS = 384
SCAN_GROUP = 4
SCAN_SEQ = 4


def _cparams(*sem):
    return pltpu.CompilerParams(dimension_semantics=sem, vmem_limit_bytes=VMEM_LIMIT_BYTES)


def _split2(a):
    hi = a.astype(BF16)
    lo = (a - hi.astype(F32)).astype(BF16)
    return hi, lo


def _dot(a, b):
    return jnp.dot(a, b, preferred_element_type=F32)


def _dot_nt(a, b):
    return lax.dot_general(a, b, (((1,), (1,)), ((), ())), preferred_element_type=F32)


def _dot_tn(a, b):
    return lax.dot_general(a, b, (((0,), (0,)), ((), ())), preferred_element_type=F32)


def _dot3(a, b_hi, b_lo, dot=_dot):
    a_hi, a_lo = _split2(a)
    return dot(a_hi, b_hi) + dot(a_hi, b_lo) + dot(a_lo, b_hi)


def _dot3f(a, b, dot=_dot):
    b_hi, b_lo = _split2(b)
    return _dot3(a, b_hi, b_lo, dot)


def _segsum(a, ones_bf16):
    a_hi, a_lo = _split2(a)
    return _dot(a_hi, ones_bf16) + _dot(a_lo, ones_bf16)


def _sigmoid(x):
    return 1.0 / (1.0 + jnp.exp(-x))


def _select_mod(mod_ref, k, is_ctx):
    return jnp.where(is_ctx, mod_ref[0, k:k + 1, :], mod_ref[1, k:k + 1, :])


def _mods_kernel(c_ref, w_ref, b_ref, o_ref):
    cv = c_ref[...]
    s = cv * _sigmoid(cv)
    o_ref[...] = _dot3f(s, w_ref[...]) + b_ref[...]


def _mods(cvecs, w_mod, b_mod):
    D, N = w_mod.shape
    tn = N // 4
    return pl.pallas_call(
        _mods_kernel,
        grid=(N // tn,),
        in_specs=[pl.BlockSpec((SUBLANES, D), lambda j: (0, 0)),
                  pl.BlockSpec((D, tn), lambda j: (0, j)),
                  pl.BlockSpec((1, tn), lambda j: (0, j))],
        out_specs=pl.BlockSpec((SUBLANES, tn), lambda j: (0, j)),
        out_shape=jax.ShapeDtypeStruct((SUBLANES, N), F32),
        compiler_params=_cparams("arbitrary"),
    )(cvecs, w_mod, b_mod.reshape(1, N))


def _inproj_kernel(ctx_len, x_ref, mod_ref, nw_ref, w_ref, o_ref):
    tm = x_ref.shape[0]
    row = pl.program_id(2) * tm + lax.broadcasted_iota(jnp.int32, (tm, 1), 0)
    is_ctx = row < ctx_len
    x = x_ref[...]
    h = x * lax.rsqrt(jnp.mean(x * x, axis=-1, keepdims=True) + NORM_EPS) * nw_ref[...]
    h = h * (1.0 + _select_mod(mod_ref, 1, is_ctx)) + _select_mod(mod_ref, 0, is_ctx)
    o_ref[...] = _dot(h.astype(BF16), w_ref[...])


def _inproj(xs, mods, norm_w, w_packed, ctx_len, tm):
    B, Tc, D = xs.shape
    N = w_packed.shape[1]
    tn = N // 2
    return pl.pallas_call(
        functools.partial(_inproj_kernel, ctx_len),
        grid=(N // tn, B, Tc // tm),
        in_specs=[pl.BlockSpec((None, tm, D), lambda n, b, i: (b, i, 0)),
                  pl.BlockSpec((None, 2, SUBLANES, D), lambda n, b, i: (b, 0, 0, 0)),
                  pl.BlockSpec((1, D), lambda n, b, i: (0, 0)),
                  pl.BlockSpec((D, tn), lambda n, b, i: (0, n))],
        out_specs=pl.BlockSpec((None, tm, tn), lambda n, b, i: (b, i, n)),
        out_shape=jax.ShapeDtypeStruct((B, Tc, N), F32),
        compiler_params=_cparams("arbitrary", "arbitrary", "arbitrary"),
    )(xs, mods, norm_w.reshape(1, D), w_packed)


def _rwkv_prep_kernel(ctx_len, total_len, u_ref, up_ref, un_ref, cw_ref, vec_ref,
                      g2h_ref, g2l_ref, dwh_ref, dwl_ref, iah_ref, ial_ref, ones_ref,
                      p_ref, pc_ref):
    tm = u_ref.shape[0]
    C = SCAN_CHUNK
    W = RWKV_WIDTH
    t0 = pl.program_id(1) * tm
    prev_ok = jnp.logical_and(t0 != 0, t0 != ctx_len).astype(F32)
    next_ok = jnp.logical_and(t0 + tm != ctx_len, t0 + tm != total_len).astype(F32)
    row = lax.broadcasted_iota(jnp.int32, (tm, 1), 0)

    u = u_ref[...]
    prev_row = up_ref[SUBLANES - 1:SUBLANES, :] * prev_ok
    next_row = un_ref[0:1, :] * next_ok
    up = jnp.where(row == 0, prev_row, pltpu.roll(u, 1, 0))
    dn = jnp.where(row == tm - 1, next_row, pltpu.roll(u, tm - 1, 0))
    uc = up * cw_ref[0:1, :] + u * cw_ref[1:2, :] + dn * cw_ref[2:3, :]

    r = uc[:, 0:W]
    k = uc[:, W:2 * W]
    v = uc[:, 2 * W:3 * W]
    ones = ones_ref[...]
    key_k = vec_ref[4:5, :]
    key_a = vec_ref[5:6, :]
    rho = vec_ref[6:7, :]

    g = _dot3(_sigmoid(uc[:, RW_GLO:RW_GLO + 256]), g2h_ref[...], g2l_ref[...])
    dl = jnp.tanh(uc[:, RW_DLO:RW_DLO + 128])
    al = uc[:, RW_ALO:RW_ALO + 128]

    kk = k * key_k
    kk = kk * lax.rsqrt(_segsum(kk * kk, ones) + 1e-12)

    ti = lax.broadcasted_iota(jnp.int32, (tm, tm), 0)
    si = lax.broadcasted_iota(jnp.int32, (tm, tm), 1)
    same_chunk = (ti // C) == (si // C)
    tri = (jnp.logical_and(same_chunk, si <= ti).astype(BF16),
           jnp.logical_and(same_chunk, si >= ti).astype(BF16))

    streams = [v, g, None]
    krep_sum = None
    for z in range(2):
        d = vec_ref[z:z + 1, :] + _dot3(dl, dwh_ref[z], dwl_ref[z])
        logw = -DECAY_SCALE * _sigmoid(d)
        a = _sigmoid(vec_ref[2 + z:3 + z, :] + _dot3(al, iah_ref[z], ial_ref[z]))
        krep = k * (1.0 + (a - 1.0) * key_a)
        ka = kk * a
        krep_sum = krep if krep_sum is None else krep_sum + krep
        l_hi = logw.astype(BF16)
        rem = logw - l_hi.astype(F32)
        l_mid = rem.astype(BF16)
        l_lo = (rem - l_mid.astype(F32)).astype(BF16)
        cum = _dot(tri[z], l_hi) + _dot(tri[z], l_mid) + _dot(tri[z], l_lo)
        e_neg = jnp.exp(-cum)
        streams += [kk * jnp.exp(cum - logw), ka * e_neg, krep * e_neg, r * jnp.exp(cum)]
        for j in range(tm // C):
            last = j * C + (C - 1 if z == 0 else 0)
            pc = jnp.exp(cum[last:last + 1, :])
            for h in range(RWKV_HEADS):
                pc_ref[z, j, h:h + 1, :] = pc[:, h * HEAD_DIM:(h + 1) * HEAD_DIM]
    bonus = _segsum(r * krep_sum * rho, ones)
    streams[2] = bonus * v
    for s, val in enumerate(streams):
        for h in range(RWKV_HEADS):
            p_ref[s, h] = val[:, h * HEAD_DIM:(h + 1) * HEAD_DIM]


def _rwkv_prep(U, conv_packed, vecs, g2, dw2, ia2, ones, ctx_len, tm):
    B, Tc, _ = U.shape
    C = SCAN_CHUNK
    H = RWKV_HEADS
    rb = RW_OFF // RW_COLS
    nt = Tc // tm
    hb = tm // SUBLANES
    full = lambda a: pl.BlockSpec(a.shape, lambda b, i: (0,) * a.ndim)
    args = (conv_packed, vecs, g2[0], g2[1], dw2[0], dw2[1], ia2[0], ia2[1], ones)
    return pl.pallas_call(
        functools.partial(_rwkv_prep_kernel, ctx_len, Tc),
        grid=(B, nt),
        in_specs=[pl.BlockSpec((None, tm, RW_COLS), lambda b, i: (b, i, rb)),
                  pl.BlockSpec((None, SUBLANES, RW_COLS),
                               lambda b, i: (b, jnp.maximum(i * hb - 1, 0), rb)),
                  pl.BlockSpec((None, SUBLANES, RW_COLS),
                               lambda b, i: (b, jnp.minimum((i + 1) * hb, Tc // SUBLANES - 1), rb))]
                 + [full(a) for a in args],
        out_specs=[pl.BlockSpec((None, N_SLOTS, H, tm, HEAD_DIM), lambda b, i: (b, 0, 0, i, 0)),
                   pl.BlockSpec((None, 2, tm // C, H, HEAD_DIM), lambda b, i: (b, 0, i, 0, 0))],
        out_shape=[jax.ShapeDtypeStruct((B, N_SLOTS, H, Tc, HEAD_DIM), F32),
                   jax.ShapeDtypeStruct((B, 2, Tc // C, H, HEAD_DIM), F32)],
        compiler_params=_cparams("arbitrary", "arbitrary"),
    )(U, U, U, *args)


def _scan_local_kernel(v_ref, a_ref, b_ref, k_ref, r_ref, pc_ref, rb_ref, yb_ref, m_ref, hm_ref):
    C = SCAN_CHUNK
    rev = pl.program_id(1) == 1
    t2 = lax.broadcasted_iota(jnp.int32, (C, 2 * C), 0)
    s2 = lax.broadcasted_iota(jnp.int32, (C, 2 * C), 1)
    right = s2 >= C
    s2 = jnp.where(right, s2 - C, s2)
    d2 = jnp.where(rev, t2 - s2, s2 - t2)
    strict_right = jnp.logical_and(d2 < 0, right)
    incl2 = d2 <= 0
    t1 = lax.broadcasted_iota(jnp.int32, (C, C), 0)
    s1 = lax.broadcasted_iota(jnp.int32, (C, C), 1)
    strict1 = jnp.where(rev, t1 - s1, s1 - t1) < 0
    ident = (t1 == s1).astype(F32)

    units = [(c, h) for c in range(SCAN_GROUP) for h in range(RWKV_HEADS)]
    n = range(len(units))
    rows = lambda ref, c, h: ref[h, c * C:(c + 1) * C, :]
    V = [rows(v_ref, c, h) for c, h in units]
    A = [rows(a_ref, c, h) for c, h in units]
    Bm = [rows(b_ref, c, h) for c, h in units]
    R = [rows(r_ref, c, h) for c, h in units]
    AR = [jnp.concatenate([A[i], R[i]], axis=0).astype(BF16) for i in n]
    BK = [jnp.concatenate([Bm[i], rows(k_ref, c, h)], axis=0).astype(BF16) for i, (c, h) in enumerate(units)]
    pc = [pc_ref[c, h:h + 1, :] for c, h in units]
    mm = lambda a, b, dot=_dot: dot(a.astype(BF16), b.astype(BF16))
    G = [_dot_nt(AR[i], BK[i]) for i in n]
    n_ab = [jnp.where(strict1, G[i][:C, :C], 0.0) for i in n]
    n_ak_wide = [jnp.where(strict_right, G[i][:C], 0.0) for i in n]
    n_r_wide = [jnp.where(incl2, jnp.where(right, G[i][C:], -G[i][C:]), 0.0) for i in n]
    nv = [mm(n_ak_wide[i], jnp.concatenate([V[i], V[i]], axis=0)) for i in n]

    inv = [ident - n_ab[i] for i in n]
    power = n_ab
    for _ in range(5):
        pb = [power[i].astype(BF16) for i in n]
        power = [_dot(pb[i], pb[i]) for i in n]
        inv = [_dot(inv[i].astype(BF16), (ident + power[i]).astype(BF16)) for i in n]

    ab = [mm(inv[i], A[i]) for i in n]
    ub = [mm(inv[i], nv[i]) for i in n]
    for i, (c, h) in enumerate(units):
        rb_ref[h, c * C:(c + 1) * C, :] = R[i] + mm(n_r_wide[i][:, :C], ab[i])
    for i, (c, h) in enumerate(units):
        yb_ref[h, c * C:(c + 1) * C, :] = mm(n_r_wide[i], jnp.concatenate([ub[i], V[i]], axis=0))
    for i, (c, h) in enumerate(units):
        m_ref[c, h] = (ident - mm(ab[i], Bm[i], _dot_tn)) * pc[i]
    for i, (c, h) in enumerate(units):
        W = jnp.concatenate([-ub[i], V[i]], axis=0).astype(BF16)
        hm_ref[c, h] = _dot_tn(W, BK[i]) * pc[i]


def _scan_seq_kernel(rbf_ref, ybf_ref, mf_ref, hmf_ref, rbr_ref, ybr_ref, mr_ref, hmr_ref,
                     yf_ref, yr_ref, state_ref):
    C = SCAN_CHUNK

    @pl.when(pl.program_id(1) == 0)
    def _():
        state_ref[...] = jnp.zeros_like(state_ref)

    dirs = ((rbf_ref, ybf_ref, mf_ref, hmf_ref, yf_ref), (rbr_ref, ybr_ref, mr_ref, hmr_ref, yr_ref))
    chains = [(z, h) for z in range(2) for h in range(RWKV_HEADS)]
    S = [state_ref[z, h] for z, h in chains]
    for step in range(SCAN_SEQ):
        Ss = [_split2(s) for s in S]
        nxt = []
        for i, (z, h) in enumerate(chains):
            rb_ref, yb_ref, m_ref, hm_ref, y_ref = dirs[z]
            c = step if z == 0 else SCAN_SEQ - 1 - step
            rows = slice(c * C, (c + 1) * C)
            y_ref[h, rows, :] = _dot_nt(rb_ref[h, rows, :].astype(BF16), Ss[i][0]) + yb_ref[h, rows, :]
            m_hi, m_lo = _split2(m_ref[c, h])
            nxt.append(_dot(Ss[i][0], m_hi) + _dot(Ss[i][0], m_lo) + _dot(Ss[i][1], m_hi) + hm_ref[c, h])
        S = nxt
    for i, (z, h) in enumerate(chains):
        state_ref[z, h] = S[i]


def _scan(P, PC, ctx_len):
    B, _, H, Tc, _ = P.shape
    C = SCAN_CHUNK
    nch = Tc // C
    assert nch % SCAN_GROUP == 0 and nch % SCAN_SEQ == 0 and (ctx_len // C) % SCAN_SEQ == 0
    gc = SCAN_GROUP * C
    tok = lambda slot: pl.BlockSpec((None, None, H, gc, HEAD_DIM), lambda b, z, j: (b, slot(z), 0, j, 0))
    mat_shape = jax.ShapeDtypeStruct((2, B, nch, H, HEAD_DIM, HEAD_DIM), F32)
    tok_shape = jax.ShapeDtypeStruct((2, B, H, Tc, HEAD_DIM), F32)
    RB, YB, MM, HM = pl.pallas_call(
        _scan_local_kernel,
        grid=(B, 2, nch // SCAN_GROUP),
        in_specs=[tok(lambda z: 0), tok(lambda z: 3 + 4 * z), tok(lambda z: 4 + 4 * z), tok(lambda z: 5 + 4 * z),
                  tok(lambda z: 6 + 4 * z),
                  pl.BlockSpec((None, None, SCAN_GROUP, H, HEAD_DIM), lambda b, z, j: (b, z, j, 0, 0))],
        out_specs=[pl.BlockSpec((None, None, H, gc, HEAD_DIM), lambda b, z, j: (z, b, 0, j, 0)),
                   pl.BlockSpec((None, None, H, gc, HEAD_DIM), lambda b, z, j: (z, b, 0, j, 0)),
                   pl.BlockSpec((None, None, SCAN_GROUP, H, HEAD_DIM, HEAD_DIM), lambda b, z, j: (z, b, j, 0, 0, 0)),
                   pl.BlockSpec((None, None, SCAN_GROUP, H, HEAD_DIM, HEAD_DIM), lambda b, z, j: (z, b, j, 0, 0, 0))],
        out_shape=[tok_shape, tok_shape, mat_shape, mat_shape],
        compiler_params=_cparams("arbitrary", "arbitrary", "arbitrary"),
    )(P, P, P, P, P, PC)

    nblk = nch // SCAN_SEQ
    nctx = ctx_len // C // SCAN_SEQ

    def block(z, j):
        return j if z == 0 else jnp.where(j < nctx, nctx - 1 - j, nblk + nctx - 1 - j)

    sc = SCAN_SEQ * C
    tok_spec = lambda z: pl.BlockSpec((None, None, H, sc, HEAD_DIM), lambda b, j: (z, b, 0, block(z, j), 0))
    mat_spec = lambda z: pl.BlockSpec((None, None, SCAN_SEQ, H, HEAD_DIM, HEAD_DIM),
                                      lambda b, j: (z, b, block(z, j), 0, 0, 0))
    out_spec = lambda z: pl.BlockSpec((None, H, sc, HEAD_DIM), lambda b, j: (b, 0, block(z, j), 0))
    y_shape = jax.ShapeDtypeStruct((B, H, Tc, HEAD_DIM), F32)
    return pl.pallas_call(
        _scan_seq_kernel,
        grid=(B, nblk),
        in_specs=[tok_spec(0), tok_spec(0), mat_spec(0), mat_spec(0),
                  tok_spec(1), tok_spec(1), mat_spec(1), mat_spec(1)],
        out_specs=[out_spec(0), out_spec(1)],
        out_shape=[y_shape, y_shape],
        scratch_shapes=[pltpu.VMEM((2, H, HEAD_DIM, HEAD_DIM), F32)],
        compiler_params=_cparams("arbitrary", "arbitrary"),
    )(RB, YB, MM, HM, RB, YB, MM, HM)


def _attn_prep_kernel(uq_ref, ukv_ref, cos_ref, sin_ref, qn_ref, kn_ref, ones_ref, q_ref, k_ref, v_ref):
    ones = ones_ref[...]
    cos = cos_ref[...]
    sin = sin_ref[...]
    lane = lax.broadcasted_iota(jnp.int32, (1, ATTN_WIDTH), 1)
    first_half = (lane % (2 * ROPE_PAIRS)) < ROPE_PAIRS

    def norm_rope(x, gain, width):
        ms = _segsum(x * x, ones[:width, :width]) * (1.0 / HEAD_DIM)
        y = x * lax.rsqrt(ms + NORM_EPS) * gain
        partner = jnp.where(first_half[:, :width],
                            pltpu.roll(y, width - ROPE_PAIRS, 1), pltpu.roll(y, ROPE_PAIRS, 1))
        return y * cos[:, :width] + partner * sin[:, :width]

    q = norm_rope(uq_ref[...], qn_ref[...], ATTN_WIDTH) * (ATTN_SCALE * math.log2(math.e))
    ukv = ukv_ref[...]
    k = norm_rope(ukv[:, :KV_WIDTH], kn_ref[:, :KV_WIDTH], KV_WIDTH)
    v = ukv[:, KV_WIDTH:]
    low_lanes = lax.broadcasted_iota(jnp.int32, (1, KV_WIDTH), 1) < HEAD_DIM
    for g in range(KV_HEADS):
        for j in range(Q_PER_KV):
            o = (g * Q_PER_KV + j) * HEAD_DIM
            q_ref[g, j] = q[:, o:o + HEAD_DIM].astype(BF16)
        k_ref[g] = k[:, g * HEAD_DIM:(g + 1) * HEAD_DIM].astype(BF16)
        v_g = v if g == 0 else pltpu.roll(v, (KV_HEADS - g) * HEAD_DIM, 1)
        v_ref[g] = jnp.where(low_lanes, v_g, 1.0).astype(BF16)


def _attn_prep(U, cos, sin, qn, kn, ones, tm):
    B, Tc, _ = U.shape
    full = lambda a: pl.BlockSpec(a.shape, lambda b, i: (0,) * a.ndim)
    return pl.pallas_call(
        _attn_prep_kernel,
        grid=(B, Tc // tm),
        in_specs=[pl.BlockSpec((None, tm, ATTN_WIDTH), lambda b, i: (b, i, Q_OFF // ATTN_WIDTH)),
                  pl.BlockSpec((None, tm, 2 * KV_WIDTH), lambda b, i: (b, i, KV_OFF // (2 * KV_WIDTH))),
                  pl.BlockSpec((tm, ATTN_WIDTH), lambda b, i: (i, 0)),
                  pl.BlockSpec((tm, ATTN_WIDTH), lambda b, i: (i, 0)),
                  full(qn), full(kn), full(ones)],
        out_specs=[pl.BlockSpec((None, KV_HEADS, Q_PER_KV, tm, HEAD_DIM), lambda b, i: (b, 0, 0, i, 0)),
                   pl.BlockSpec((None, KV_HEADS, tm, HEAD_DIM), lambda b, i: (b, 0, i, 0)),
                   pl.BlockSpec((None, KV_HEADS, tm, 2 * HEAD_DIM), lambda b, i: (b, 0, i, 0))],
        out_shape=[jax.ShapeDtypeStruct((B, KV_HEADS, Q_PER_KV, Tc, HEAD_DIM), BF16),
                   jax.ShapeDtypeStruct((B, KV_HEADS, Tc, HEAD_DIM), BF16),
                   jax.ShapeDtypeStruct((B, KV_HEADS, Tc, 2 * HEAD_DIM), BF16)],
        compiler_params=_cparams("arbitrary", "arbitrary"),
    )(U, U, cos, sin, qn, kn, ones)


def _flash_kernel(ctx_len, q_ref, k_ref, v_ref, o_ref, m_ref, acc_ref):
    G, tq, _ = q_ref.shape
    tk = k_ref.shape[0]
    qi = pl.program_id(2)
    ki = pl.program_id(3)
    has_ctx_rows = qi * tq < ctx_len

    @pl.when(ki == 0)
    def _():
        m_ref[...] = jnp.full_like(m_ref, -jnp.inf)
        acc_ref[...] = jnp.zeros_like(acc_ref)

    def step(masked):
        k = k_ref[...]
        v = v_ref[...]
        rows = FLASH_UNIT_ROWS
        units = [(j, r0) for j in range(G) for r0 in range(0, tq, rows)]

        def logits(u):
            j, r0 = u
            s = _dot_nt(q_ref[j, r0:r0 + rows, :], k)
            if masked:
                key = ki * tk + lax.broadcasted_iota(jnp.int32, (1, tk), 1)
                row = qi * tq + r0 + lax.broadcasted_iota(jnp.int32, (rows, 1), 0)
                s = jnp.where(jnp.logical_or(row >= ctx_len, key < ctx_len), s, -jnp.inf)
            return s

        s_next = logits(units[0])
        for n, (j, r0) in enumerate(units):
            s = s_next
            if n + 1 < len(units):
                s_next = logits(units[n + 1])
            m_old = m_ref[j, r0:r0 + rows, :]
            m_new = jnp.maximum(m_old, jnp.max(s, axis=-1, keepdims=True))
            p = jnp.exp2(s - m_new).astype(BF16)
            acc_ref[j, r0:r0 + rows, :] = jnp.exp2(m_old - m_new) * acc_ref[j, r0:r0 + rows, :] + _dot(p, v)
            m_ref[j, r0:r0 + rows, :] = m_new

    @pl.when(jnp.logical_not(has_ctx_rows))
    def _():
        step(False)

    @pl.when(has_ctx_rows)
    def _():
        step(True)

    @pl.when(ki == pl.num_programs(3) - 1)
    def _():
        for j in range(G):
            acc = acc_ref[j]
            o_ref[j] = acc[:, :HEAD_DIM] / acc[:, HEAD_DIM:]


def _flash(q, k, v, ctx_len, tq, tk):
    B, KVH, G, Tc, Dh = q.shape
    return pl.pallas_call(
        functools.partial(_flash_kernel, ctx_len),
        grid=(B, KVH, Tc // tq, Tc // tk),
        in_specs=[pl.BlockSpec((None, None, G, tq, Dh), lambda b, g, i, j: (b, g, 0, i, 0)),
                  pl.BlockSpec((None, None, tk, Dh), lambda b, g, i, j: (b, g, j, 0)),
                  pl.BlockSpec((None, None, tk, 2 * Dh), lambda b, g, i, j: (b, g, j, 0))],
        out_specs=pl.BlockSpec((None, None, G, tq, Dh), lambda b, g, i, j: (b, g, 0, i, 0)),
        out_shape=jax.ShapeDtypeStruct((B, KVH, G, Tc, Dh), F32),
        scratch_shapes=[pltpu.VMEM((G, tq, 1), F32), pltpu.VMEM((G, tq, 2 * Dh), F32)],
        compiler_params=_cparams("arbitrary", "arbitrary", "arbitrary", "arbitrary"),
    )(q, k, v)


def _merge_kernel(ctx_len, x_ref, mod_ref, yf_ref, yr_ref, g_ref, bv_ref, o_ref, gate_ref, gn_ref, nw_ref,
                  wr_ref, wa_ref, wo_ref, rth_ref, rtl_ref, rtth_ref, rttl_ref,
                  x1_ref, h2_ref, aff_ref, afft_ref):
    tm = x_ref.shape[0]
    D = x_ref.shape[1]
    row = pl.program_id(1) * tm + lax.broadcasted_iota(jnp.int32, (tm, 1), 0)
    is_ctx = row < ctx_len

    rw = []
    for h in range(RWKV_HEADS):
        y = yf_ref[h] + yr_ref[h]
        mu = jnp.mean(y, axis=-1, keepdims=True)
        yc = y - mu
        var = jnp.mean(yc * yc, axis=-1, keepdims=True)
        yn = yc * lax.rsqrt(var + GN_EPS) * gn_ref[0, h:h + 1, :] + gn_ref[1, h:h + 1, :]
        rw.append((yn + bv_ref[h]) * g_ref[h])
    y_rwkv = jnp.concatenate(rw, axis=-1).astype(BF16)
    y_attn = jnp.concatenate([o_ref[g, j] for g in range(KV_HEADS) for j in range(Q_PER_KV)],
                             axis=-1).astype(BF16)
    gates = gate_ref[...]
    merged = (_sigmoid(gates[:, :D]) * _dot(y_rwkv, wr_ref[...])
              + _sigmoid(gates[:, D:]) * _dot(y_attn, wa_ref[...]))
    x1 = x_ref[...] + _select_mod(mod_ref, 2, is_ctx) * _dot(merged.astype(BF16), wo_ref[...])
    x1_ref[...] = x1

    h2 = x1 * lax.rsqrt(jnp.mean(x1 * x1, axis=-1, keepdims=True) + NORM_EPS) * nw_ref[...]
    h2 = h2 * (1.0 + _select_mod(mod_ref, 4, is_ctx)) + _select_mod(mod_ref, 3, is_ctx)
    h2_ref[...] = h2
    h_hi, h_lo = _split2(h2)
    rt_hi, rt_lo = rth_ref[...], rtl_ref[...]
    logits = _dot(h_hi, rt_hi) + _dot(h_hi, rt_lo) + _dot(h_lo, rt_hi)
    lane = lax.broadcasted_iota(jnp.int32, (1, LANES), 1)
    logits = jnp.where(lane < N_EXPERTS, logits, -jnp.inf)
    e = jnp.exp(logits - jnp.max(logits, axis=-1, keepdims=True))
    aff_ref[...] = e / jnp.sum(e, axis=-1, keepdims=True)
    rtt_hi, rtt_lo = rtth_ref[...], rttl_ref[...]
    logits_t = _dot_nt(rtt_hi, h_hi) + _dot_nt(rtt_hi, h_lo) + _dot_nt(rtt_lo, h_hi)
    et = jnp.exp(logits_t - jnp.max(logits_t, axis=0, keepdims=True))
    afft_ref[...] = et / jnp.sum(et, axis=0, keepdims=True)


def _merge(xs, mods, Y, P, O, U, gn, norm_w, wr, wa, wo, rt, rtt, ctx_len, tm):
    B, Tc, D = xs.shape
    H = RWKV_HEADS
    full = lambda a: pl.BlockSpec(a.shape, lambda b, i: (0,) * a.ndim)
    nw = norm_w.reshape(1, D)
    return pl.pallas_call(
        functools.partial(_merge_kernel, ctx_len),
        grid=(B, Tc // tm),
        in_specs=[pl.BlockSpec((None, tm, D), lambda b, i: (b, i, 0)),
                  pl.BlockSpec((None, 2, SUBLANES, D), lambda b, i: (b, 0, 0, 0)),
                  pl.BlockSpec((None, H, tm, HEAD_DIM), lambda b, i: (b, 0, i, 0)),
                  pl.BlockSpec((None, H, tm, HEAD_DIM), lambda b, i: (b, 0, i, 0)),
                  pl.BlockSpec((None, None, H, tm, HEAD_DIM), lambda b, i: (b, 1, 0, i, 0)),
                  pl.BlockSpec((None, None, H, tm, HEAD_DIM), lambda b, i: (b, 2, 0, i, 0)),
                  pl.BlockSpec((None, KV_HEADS, Q_PER_KV, tm, HEAD_DIM), lambda b, i: (b, 0, 0, i, 0)),
                  pl.BlockSpec((None, tm, GATE_COLS), lambda b, i: (b, i, GATE_OFF // GATE_COLS)),
                  full(gn), full(nw), full(wr), full(wa), full(wo), full(rt[0]), full(rt[1]),
                  full(rtt[0]), full(rtt[1])],
        out_specs=[pl.BlockSpec((None, tm, D), lambda b, i: (b, i, 0)),
                   pl.BlockSpec((None, tm, D), lambda b, i: (b, i, 0)),
                   pl.BlockSpec((None, tm, LANES), lambda b, i: (b, i, 0)),
                   pl.BlockSpec((None, N_EXPERTS, tm), lambda b, i: (b, 0, i))],
        out_shape=[jax.ShapeDtypeStruct((B, Tc, D), F32),
                   jax.ShapeDtypeStruct((B, Tc, D), F32),
                   jax.ShapeDtypeStruct((B, Tc, LANES), F32),
                   jax.ShapeDtypeStruct((B, N_EXPERTS, Tc), F32)],
        compiler_params=_cparams("arbitrary", "arbitrary"),
    )(xs, mods, Y[0], Y[1], P, P, O, U, gn, nw, wr, wa, wo, rt[0], rt[1], rtt[0], rtt[1])


def _route_kernel(start, n, cap, tile, afft_ref, idx_ref, tab_ref, bits_ref, cnt_ref, sel_ref):
    E, Tc = afft_ref.shape
    nblk = (start + n) // LANES
    tok = lax.broadcasted_iota(jnp.int32, (1, Tc), 1)
    in_set = jnp.logical_and(tok >= start, tok < start + n)
    bits_ref[...] = jnp.where(in_set, lax.bitcast_convert_type(afft_ref[...], jnp.int32), -1)

    def search(i, thr):
        cand = jnp.bitwise_or(thr, jnp.left_shift(1, 30 - i))
        count = jnp.sum(jnp.where(bits_ref[...] >= cand, 1.0, 0.0), axis=1, keepdims=True)
        return jnp.where(count >= cap, cand, thr)

    thr = lax.fori_loop(0, 31, search, jnp.zeros((E, 1), jnp.int32))
    above = jnp.sum(jnp.where(bits_ref[...] > thr, 1.0, 0.0), axis=1, keepdims=True)
    need = cap - above

    r = lax.broadcasted_iota(jnp.int32, (LANES, LANES), 0)
    c = lax.broadcasted_iota(jnp.int32, (LANES, LANES), 1)
    tri = (r <= c).astype(BF16)
    ties_before = jnp.zeros((E, 1), F32)
    taken_before = jnp.zeros((E, 1), F32)
    for j in range(nblk):
        blk = bits_ref[:, j * LANES:(j + 1) * LANES]
        tie = jnp.where(blk == thr, 1.0, 0.0)
        tie_rank = ties_before + _dot(tie.astype(BF16), tri) - tie
        sel = jnp.where(jnp.logical_or(blk > thr, jnp.logical_and(blk == thr, tie_rank < need)), 1.0, 0.0)
        cnt_ref[:, j * LANES:(j + 1) * LANES] = taken_before + _dot(sel.astype(BF16), tri)
        sel_ref[:, j * LANES:(j + 1) * LANES] = sel
        ties_before = ties_before + jnp.sum(tie, axis=1, keepdims=True)
        taken_before = taken_before + jnp.sum(sel, axis=1, keepdims=True)
    if nblk * LANES < Tc:
        cnt_ref[:, nblk * LANES:] = jnp.full((E, Tc - nblk * LANES), float(cap), F32)
        sel_ref[:, nblk * LANES:] = jnp.zeros((E, Tc - nblk * LANES), F32)

    t = lax.broadcasted_iota(jnp.int32, (Tc, LANES), 0)
    k = lax.broadcasted_iota(jnp.int32, (Tc, LANES), 1)
    tab_ref[...] = _dot(sel_ref[...].astype(BF16), (t < k * tile).astype(BF16)).astype(jnp.int32)

    sc = min(cap, 256)
    slot = lax.broadcasted_iota(jnp.int32, (sc, 1), 0).astype(F32)

    for e in range(E):
        for ci in range(cap // sc):
            p_col = slot + float(ci * sc)

            def per_block(j, acc, e=e, p_col=p_col):
                off = pl.multiple_of(j * LANES, LANES)
                return acc + jnp.where(cnt_ref[e:e + 1, pl.ds(off, LANES)] <= p_col, 1.0, 0.0)

            acc = lax.fori_loop(0, nblk, per_block, jnp.zeros((sc, LANES), F32))
            idx_ref[e, ci * sc:(ci + 1) * sc, :] = jnp.sum(acc, axis=1, keepdims=True).astype(jnp.int32)


def _route(afft, start, n, cap, tile):
    B, E, Tc = afft.shape
    return pl.pallas_call(
        functools.partial(_route_kernel, start, n, cap, tile),
        grid=(B,),
        in_specs=[pl.BlockSpec((None, E, Tc), lambda b: (b, 0, 0))],
        out_specs=[pl.BlockSpec((None, E, cap, 1), lambda b: (b, 0, 0, 0)),
                   pl.BlockSpec((None, E, LANES), lambda b: (b, 0, 0))],
        out_shape=[jax.ShapeDtypeStruct((B, E, cap, 1), jnp.int32),
                   jax.ShapeDtypeStruct((B, E, LANES), jnp.int32)],
        scratch_shapes=[pltpu.VMEM((E, Tc), jnp.int32), pltpu.VMEM((E, Tc), F32), pltpu.VMEM((E, Tc), F32)],
        compiler_params=_cparams("arbitrary"),
    )(afft)


def _ffn_kernel(n_steps, rows_ref, h_hbm, wg_ref, wu_ref, wd_ref, o_ref, xf_ref, xb_ref, acc_ref, sem):
    f = pl.program_id(2)
    tm = xb_ref.shape[0]
    tile = pl.program_id(0) * pl.num_programs(1) + pl.program_id(1)
    last_tile = pl.num_programs(0) * pl.num_programs(1) - 1
    slot = tile % 2

    def row_copy(buf, r, src_row):
        return pltpu.make_async_copy(h_hbm.at[pl.ds(src_row, 1)], xf_ref.at[buf, pl.ds(r, 1)], sem.at[buf])

    @pl.when(jnp.logical_and(tile == 0, f == 0))
    def _():
        def start(r, carry):
            row_copy(0, r, rows_ref[r]).start()
            return carry
        lax.fori_loop(0, tm, start, 0)

    next_base = jnp.minimum(tile + 1, last_tile) * tm
    per_step = tm // n_steps

    @pl.when(f == 0)
    def _():
        for r in range(tm):
            row_copy(slot, r, 0).wait()
        xb_ref[...] = xf_ref[slot].astype(BF16)
        acc_ref[...] = jnp.zeros_like(acc_ref)
        for r in range(per_step * n_steps, tm):
            row_copy(1 - slot, r, rows_ref[next_base + r]).start()

    for i in range(per_step):
        r = i * n_steps + f
        row_copy(1 - slot, r, rows_ref[next_base + r]).start()

    x = xb_ref[...]
    a = _dot(x, wg_ref[...].astype(BF16))
    b = _dot(x, wu_ref[...].astype(BF16))
    hid = (a * _sigmoid(a) * b).astype(BF16)
    acc_ref[...] += _dot(hid, wd_ref[...].astype(BF16))

    @pl.when(f == n_steps - 1)
    def _():
        o_ref[...] = acc_ref[...]

    @pl.when(jnp.logical_and(tile == last_tile, f == n_steps - 1))
    def _():
        for r in range(tm):
            row_copy(1 - slot, r, 0).wait()


def _expert_ffn(rows, h_flat, layer, w_gate, w_up, w_down, tm, tf):
    _, E, D, F = w_gate.shape
    M = rows.shape[0] // E
    n_steps = F // tf
    grid_spec = pltpu.PrefetchScalarGridSpec(
        num_scalar_prefetch=1,
        grid=(E, M // tm, n_steps),
        in_specs=[pl.BlockSpec(memory_space=pl.ANY),
                  pl.BlockSpec((None, None, D, tf), lambda e, i, f, rows: (layer, e, 0, f)),
                  pl.BlockSpec((None, None, D, tf), lambda e, i, f, rows: (layer, e, 0, f)),
                  pl.BlockSpec((None, None, tf, D), lambda e, i, f, rows: (layer, e, f, 0))],
        out_specs=pl.BlockSpec((None, tm, D), lambda e, i, f, rows: (e, i, 0)),
        scratch_shapes=[pltpu.VMEM((2, tm, D), F32), pltpu.VMEM((tm, D), BF16), pltpu.VMEM((tm, D), F32),
                        pltpu.SemaphoreType.DMA((2,))])
    return pl.pallas_call(
        functools.partial(_ffn_kernel, n_steps),
        grid_spec=grid_spec,
        out_shape=jax.ShapeDtypeStruct((E, M, D), F32),
        compiler_params=_cparams("arbitrary", "arbitrary", "arbitrary"),
    )(rows, h_flat, w_gate, w_up, w_down)


def _combine_kernel(set_start, cap, is_ctx, final, idx_ref, tab_ref, ye_hbm, x1_ref, aff_ref, mod_ref, nw_ref,
                    *rest):
    o_ref, stage_ref, sem = rest[-3:]
    E = N_EXPERTS
    tm = x1_ref.shape[0]
    b = pl.program_id(0)
    k = pl.program_id(1) + set_start // tm
    t0 = k * tm
    stage_ref[...] = jnp.zeros_like(stage_ref)

    def row_copy(e, s):
        tok = idx_ref[(b * E + e) * cap + s] - t0
        return pltpu.make_async_copy(ye_hbm.at[e, pl.ds(b * cap + s, 1)], stage_ref.at[e, pl.ds(tok, 1)], sem)

    bounds = [(tab_ref[(b * E + e) * LANES + k], tab_ref[(b * E + e) * LANES + k + 1]) for e in range(E)]
    for e in range(E):
        def start(s, carry, e=e):
            row_copy(e, s).start()
            return carry
        lax.fori_loop(bounds[e][0], bounds[e][1], start, 0)
    for e in range(E):
        count = bounds[e][1] - bounds[e][0]
        bulk = pl.multiple_of((count // SUBLANES) * SUBLANES, SUBLANES)

        @pl.when(bulk > 0)
        def _(e=e, bulk=bulk):
            pltpu.make_async_copy(ye_hbm.at[e, pl.ds(0, bulk)], stage_ref.at[e, pl.ds(0, bulk)], sem).wait()

        def wait(s, carry, e=e):
            row_copy(e, s).wait()
            return carry
        lax.fori_loop(bounds[e][0] + bulk, bounds[e][1], wait, 0)

    aff = aff_ref[...]
    total = stage_ref[0] * aff[:, 0:1]
    for e in range(1, E):
        total = total + stage_ref[e] * aff[:, e:e + 1]
    x = x1_ref[...] + mod_ref[0 if is_ctx else 1, 5:6, :] * total
    if final:
        x = x * lax.rsqrt(jnp.mean(x * x, axis=-1, keepdims=True) + NORM_EPS) * nw_ref[...]
    o_ref[...] = x


def _combine(ye, idx, tab, x1, aff, mods, norm_w, set_start, n, cap, tm, is_ctx, final, into=None):
    B, Tc, D = x1.shape
    E = N_EXPERTS
    off = set_start // tm
    in_specs = [pl.BlockSpec(memory_space=pl.ANY),
                pl.BlockSpec((None, tm, D), lambda b, i, *_: (b, i + off, 0)),
                pl.BlockSpec((None, tm, LANES), lambda b, i, *_: (b, i + off, 0)),
                pl.BlockSpec((None, 2, SUBLANES, D), lambda b, i, *_: (b, 0, 0, 0)),
                pl.BlockSpec((1, D), lambda b, i, *_: (0, 0))]
    args = [idx.reshape(-1), tab.reshape(-1), ye, x1, aff, mods, norm_w.reshape(1, D)]
    aliases = {}
    if into is not None:
        in_specs.append(pl.BlockSpec(memory_space=pl.ANY))
        aliases = {len(args): 0}
        args.append(into)
    if final:
        out_shape = jax.ShapeDtypeStruct((B, n, D), F32)
        out_spec = pl.BlockSpec((None, tm, D), lambda b, i, *_: (b, i, 0))
    else:
        out_shape = jax.ShapeDtypeStruct((B, Tc, D), F32)
        out_spec = pl.BlockSpec((None, tm, D), lambda b, i, *_: (b, i + off, 0))
    grid_spec = pltpu.PrefetchScalarGridSpec(
        num_scalar_prefetch=2, grid=(B, n // tm), in_specs=in_specs, out_specs=out_spec,
        scratch_shapes=[pltpu.VMEM((E, tm, D), F32), pltpu.SemaphoreType.DMA(())])
    return pl.pallas_call(
        functools.partial(_combine_kernel, set_start, cap, is_ctx, final),
        grid_spec=grid_spec,
        out_shape=out_shape,
        input_output_aliases=aliases,
        compiler_params=_cparams("arbitrary", "arbitrary"),
    )(*args)


def _moe_set(afft, h_flat, layer, w_gate, w_up, w_down, Tc, start, n, tile):
    B = afft.shape[0]
    E = N_EXPERTS
    cap = EC_CAPACITY * n // E
    idx, tab = _route(afft, start, n, cap, tile)
    idx = idx.reshape(B, E, cap)
    rows = idx + (jnp.arange(B, dtype=jnp.int32) * Tc)[:, None, None]
    rows = jnp.swapaxes(rows, 0, 1).reshape(-1)
    ye = _expert_ffn(rows, h_flat, layer, w_gate, w_up, w_down, min(1024, B * cap), 256)
    return ye, idx, tab, cap


def _pack_cols(w):
    W = RWKV_WIDTH
    lead = w.shape[:-1]
    zeros = lambda n: jnp.zeros(lead + (n,), w.dtype)
    rk = 3 * W
    rw_end = rk + LORA_GATE + 2 * LORA_DECAY + 2 * LORA_ICLR
    gqa_end = rw_end + ATTN_WIDTH + 2 * KV_WIDTH
    return jnp.concatenate([
        w[..., gqa_end:],
        w[..., :rk],
        w[..., rk:rk + LORA_GATE], zeros(256 - LORA_GATE),
        w[..., rk + LORA_GATE:rw_end],
        w[..., rw_end:gqa_end],
    ], axis=-1)


def _rope_tables(n_tokens, ctx_len, width):
    rows = n_tokens // GRID_W
    row = jnp.repeat(jnp.arange(rows, dtype=F32), GRID_W)
    col = jnp.broadcast_to(jnp.arange(GRID_W, dtype=F32)[None, :], (rows, GRID_W)).reshape(-1)
    inv = jnp.power(ROPE_BASE, -jnp.arange(ROPE_PAIRS, dtype=F32) / ROPE_PAIRS)
    ang = jnp.stack([row[:, None] * inv, col[:, None] * inv], axis=1)
    cos = jnp.repeat(jnp.cos(ang)[:, :, None, :], 2, axis=2).reshape(n_tokens, HEAD_DIM)
    sin = jnp.sin(ang)
    sin = jnp.stack([-sin, sin], axis=2).reshape(n_tokens, HEAD_DIM)
    cos = jnp.concatenate([jnp.ones((ctx_len, HEAD_DIM), F32), cos], axis=0)
    sin = jnp.concatenate([jnp.zeros((ctx_len, HEAD_DIM), F32), sin], axis=0)
    reps = width // HEAD_DIM
    return jnp.tile(cos, (1, reps)), jnp.tile(sin, (1, reps))


def _pad_rows(w, rows, offset=0):
    out = jnp.zeros((rows,) + w.shape[1:], w.dtype)
    return out.at[offset:offset + w.shape[0]].set(w)


def _split_param(w):
    hi = w.astype(BF16)
    return hi, (w - hi.astype(F32)).astype(BF16)


def kernel(x, c, ctx, c_ctx, w_mod, b_mod, norm_mix, norm_ffn, w_in, conv_w, decay_w0, decay_w2, iclr_a0, iclr_a2, gate_g2, key_k, key_a, bonus_rho, gn_g, gn_b, q_norm, k_norm, w_branch_rwkv, w_branch_attn, w_out, w_router, w_gate, w_up, w_down, norm_final):
    B, T, D = x.shape
    Lc = ctx.shape[1]
    depth = w_in.shape[0]
    W = RWKV_WIDTH
    H = RWKV_HEADS
    tm = ROW_TILE
    assert Lc % tm == 0 and T % tm == 0 and tm % SCAN_CHUNK == 0 and B + 1 <= SUBLANES

    xs = jnp.concatenate([ctx, x], axis=1)
    Tc = Lc + T
    cvecs = jnp.zeros((SUBLANES, D), F32).at[:B].set(c).at[B].set(c_ctx)
    cos, sin = _rope_tables(T, Lc, ATTN_WIDTH)
    head = lax.broadcasted_iota(jnp.int32, (W, W), 0) // HEAD_DIM
    ones = (head == head.T).astype(BF16)
    tq = 3 * ROW_TILE if Tc % (3 * ROW_TILE) == 0 else ROW_TILE
    tk = 22 * LANES if Tc % (22 * LANES) == 0 else tq
    inproj_tm = 3 * ROW_TILE if Tc % (3 * ROW_TILE) == 0 else ROW_TILE

    for l in range(depth):
        m = _mods(cvecs, w_mod[l], b_mod[l]).reshape(SUBLANES, 6, D)
        m = jnp.pad(m, ((0, 0), (0, SUBLANES - 6), (0, 0)))
        mods = jnp.stack([jnp.broadcast_to(m[B], (B, SUBLANES, D)), m[:B]], axis=1)

        U = _inproj(xs, mods, norm_mix[l], _pack_cols(w_in[l]).astype(BF16), Lc, inproj_tm)

        conv_packed = _pad_rows(_pack_cols(jnp.pad(conv_w[l], ((0, 0), (0, w_in.shape[2] - conv_w.shape[2]))))
                                [:, RW_OFF:RW_OFF + RW_COLS], SUBLANES)
        vecs = _pad_rows(jnp.stack([decay_w0[l, 0], decay_w0[l, 1], iclr_a0[l, 0], iclr_a0[l, 1],
                                    key_k[l], key_a[l], bonus_rho[l].reshape(W)]), SUBLANES)
        g2 = _split_param(_pad_rows(gate_g2[l], 256))
        dw2 = _split_param(jnp.stack([_pad_rows(decay_w2[l, z], 2 * LORA_DECAY, z * LORA_DECAY) for z in range(2)]))
        ia2 = _split_param(jnp.stack([_pad_rows(iclr_a2[l, z], 2 * LORA_ICLR, z * LORA_ICLR) for z in range(2)]))
        P, PC = _rwkv_prep(U, conv_packed, vecs, g2, dw2, ia2, ones, Lc, tm)
        Y = _scan(P, PC, Lc)

        qn = jnp.tile(q_norm[l], ATTN_WIDTH // HEAD_DIM).reshape(1, ATTN_WIDTH)
        kn = jnp.tile(k_norm[l], ATTN_WIDTH // HEAD_DIM).reshape(1, ATTN_WIDTH)
        q, k, v = _attn_prep(U, cos, sin, qn, kn, ones, tm)
        O = _flash(q, k, v, Lc, tq, tk)

        gn = jnp.stack([gn_g[l].reshape(H, HEAD_DIM), gn_b[l].reshape(H, HEAD_DIM)])
        rt = _split_param(jnp.pad(w_router[l], ((0, 0), (0, LANES - N_EXPERTS))))
        rtt = _split_param(w_router[l].T)
        x1, h2, aff, afft = _merge(xs, mods, Y, P, O, U, gn, norm_ffn[l], w_branch_rwkv[l].astype(BF16),
                                   w_branch_attn[l].astype(BF16), w_out[l].astype(BF16), rt, rtt, Lc, tm)

        last = l == depth - 1
        h_flat = h2.reshape(B * Tc, D)
        ye, idx, tab, cap = _moe_set(afft, h_flat, l, w_gate, w_up, w_down, Tc, Lc, T, tm)
        if last:
            xs = _combine(ye, idx, tab, x1, aff, mods, norm_final, Lc, T, cap, tm, False, True)
        else:
            xs = _combine(ye, idx, tab, x1, aff, mods, norm_final, 0, Tc, cap, tm, False, False)
            ye, idx, tab, cap = _moe_set(afft, h_flat, l, w_gate, w_up, w_down, Tc, 0, Lc, tm)
            xs = _combine(ye, idx, tab, x1, aff, mods, norm_final, 0, Lc, cap, tm, True, False, into=xs)
    return xs
```
